```python
import math
import jax, jax.numpy as jnp
from jax import lax
import numpy as np

D_MODEL = 2048
BATCH = 2
SEQ = 4096
DEPTH = 4

GRID_W = 64
CTX_LEN = 256
N_MIXERS = 2
N_MLA = (DEPTH + 1) // 2
N_GDN = DEPTH // 2
EPS = 1e-6
MLA_HEADS = 16
MLA_NOPE = 128
MLA_ROPE = 64
MLA_V = 128
MLA_QK = MLA_NOPE + MLA_ROPE
Q_LORA = 512
KV_LORA = 512
MLA_WIDTH = MLA_HEADS * MLA_V
MLA_IN = Q_LORA + KV_LORA + MLA_ROPE + MLA_WIDTH
ROPE_THETA = 10000.0
Q_BLOCK = 128
GDN_K_HEADS = 16
GDN_V_HEADS = 32
GDN_DK = 128
GDN_DV = 128
GDN_KEY = GDN_K_HEADS * GDN_DK
GDN_VAL = GDN_V_HEADS * GDN_DV
GDN_CONV_CH = 2 * GDN_KEY + GDN_VAL
GDN_IN = GDN_CONV_CH + GDN_VAL + 2 * 2 * GDN_V_HEADS
CONV_W = 5
CHUNK = 64

kernel_name = "hybrid_mla_gdn_diffusion_block"

F32 = jnp.float32


def _rmsnorm(x, w):
    xf = x.astype(F32)
    y = xf * lax.rsqrt(jnp.mean(xf * xf, axis=-1, keepdims=True) + EPS)
    return (y * w.astype(F32)).astype(x.dtype)


def _l2norm(x):
    xf = x.astype(F32)
    return xf * lax.rsqrt(jnp.sum(xf * xf, axis=-1, keepdims=True) + EPS)


def _adaln(cond, w, b):
    m = jax.nn.silu(cond) @ w + b
    return jnp.split(m, 3, axis=-1)


def _axial_rope_tables(n_tokens):
    rows = n_tokens // GRID_W
    r, col = jnp.meshgrid(jnp.arange(rows), jnp.arange(GRID_W), indexing="ij")
    r = r.reshape(-1).astype(F32)
    col = col.reshape(-1).astype(F32)
    half = MLA_ROPE // 2
    inv = ROPE_THETA ** (-jnp.arange(0, half, 2, dtype=F32) / half)
    ang = jnp.concatenate([r[:, None] * inv, col[:, None] * inv], axis=-1)
    return jnp.cos(ang), jnp.sin(ang)


def _apply_rope(x, cos, sin):
    xf = x.astype(F32)
    x1, x2 = jnp.split(xf, 2, axis=-1)
    cs, sn = cos[None, :, None, :], sin[None, :, None, :]
    return jnp.concatenate([x1 * cs - x2 * sn, x1 * sn + x2 * cs], axis=-1).astype(x.dtype)


def _mla_project(h, w_in, q_norm, w_uq, kv_norm, w_ukv):
    B, T, _ = h.shape
    proj = h @ w_in
    cq, ckv, k_rope, z = jnp.split(proj, [Q_LORA, Q_LORA + KV_LORA, Q_LORA + KV_LORA + MLA_ROPE], axis=-1)
    q = (_rmsnorm(cq, q_norm) @ w_uq).reshape(B, T, MLA_HEADS, MLA_QK)
    kv = (_rmsnorm(ckv, kv_norm) @ w_ukv).reshape(B, T, MLA_HEADS, MLA_NOPE + MLA_V)
    return (q[..., :MLA_NOPE], q[..., MLA_NOPE:], kv[..., :MLA_NOPE],
            k_rope[:, :, None, :], kv[..., MLA_NOPE:], z)


def _mla_keys(k_nope, k_rope):
    return jnp.concatenate([k_nope, jnp.broadcast_to(k_rope, k_nope.shape[:-1] + (MLA_ROPE,))], axis=-1)


def _softmax_attend(q, k, v):
    s = jnp.einsum("bqhd,bkhd->bhqk", q, k, preferred_element_type=F32) * (MLA_QK ** -0.5)
    p = jax.nn.softmax(s, axis=-1).astype(v.dtype)
    return jnp.einsum("bhqk,bkhd->bqhd", p, v)


def _blocked_attend(q, k, v):
    B, S, H, Dq = q.shape
    nb = S // Q_BLOCK
    qb = jnp.moveaxis(q.reshape(B, nb, Q_BLOCK, H, Dq), 1, 0)
    ob = lax.map(lambda qq: _softmax_attend(qq, k, v), qb)
    return jnp.moveaxis(ob, 0, 1).reshape(B, S, H, v.shape[-1])


def _mla_mixer(h, h_ctx, cos, sin, w_in, q_norm, w_uq, kv_norm, w_ukv, w_o, need_ctx_out):
    B, S, _ = h.shape
    qn, qr, kn, kr, v, z = _mla_project(h, w_in, q_norm, w_uq, kv_norm, w_ukv)
    qnc, qrc, knc, krc, vc, zc = _mla_project(h_ctx, w_in, q_norm, w_uq, kv_norm, w_ukv)
    q = jnp.concatenate([qn, _apply_rope(qr, cos, sin)], axis=-1)
    k = _mla_keys(kn, _apply_rope(kr, cos, sin))
    k_c = _mla_keys(knc, krc)
    keys = jnp.concatenate([k_c, k], axis=1)
    vals = jnp.concatenate([vc, v], axis=1)
    o = _blocked_attend(q, keys, vals).reshape(B, S, MLA_WIDTH)
    y = (o * jax.nn.silu(z)) @ w_o
    y_ctx = None
    if need_ctx_out:
        q_c = jnp.concatenate([qnc, qrc], axis=-1)
        o_c = _softmax_attend(q_c, k_c, vc).reshape(B, h_ctx.shape[1], MLA_WIDTH)
        y_ctx = (o_c * jax.nn.silu(zc)) @ w_o
    return y, y_ctx


def _depthwise_conv(x, w):
    return lax.conv_general_dilated(
        x, w[:, None, :].astype(x.dtype), window_strides=(1,),
        padding=[(CONV_W // 2, CONV_W // 2)],
        dimension_numbers=("NWC", "WIO", "NWC"), feature_group_count=x.shape[-1])


def _gdn_project(h, w_in, conv_w, a_log, dt_bias):
    B, T, _ = h.shape
    proj = h @ w_in
    qkv, z, b, a = jnp.split(proj, [GDN_CONV_CH, GDN_CONV_CH + GDN_VAL,
                                    GDN_CONV_CH + GDN_VAL + 2 * GDN_V_HEADS], axis=-1)
    qkv = jax.nn.silu(_depthwise_conv(qkv, conv_w))
    q, k, v = jnp.split(qkv, [GDN_KEY, 2 * GDN_KEY], axis=-1)
    rep = GDN_V_HEADS // GDN_K_HEADS
    q = jnp.repeat(_l2norm(q.reshape(B, T, GDN_K_HEADS, GDN_DK)), rep, axis=2)
    k = jnp.repeat(_l2norm(k.reshape(B, T, GDN_K_HEADS, GDN_DK)), rep, axis=2)
    v = v.reshape(B, T, GDN_V_HEADS, GDN_DV).astype(F32)
    beta = jax.nn.sigmoid(b.reshape(B, T, 2, GDN_V_HEADS).astype(F32))
    g = -jnp.exp(a_log.astype(F32)) * jax.nn.softplus(
        a.reshape(B, T, 2, GDN_V_HEADS).astype(F32) + dt_bias.astype(F32))
    return q, k, v, g, beta, z


def _to_chunks(x):
    B, T, H = x.shape[:3]
    x = x.reshape((B, T // CHUNK, CHUNK, H) + x.shape[3:])
    return jnp.moveaxis(x, 3, 1)


def _gdn_chunked(q, k, v, g, beta, state0):
    B, T, H, _ = q.shape
    qc = _to_chunks(q * (GDN_DK ** -0.5))
    kc = _to_chunks(k)
    vc = _to_chunks(v)
    gcum = jnp.cumsum(_to_chunks(g), axis=-1)
    bc = _to_chunks(beta)
    incl = jnp.tril(jnp.ones((CHUNK, CHUNK), bool))
    strict = jnp.tril(jnp.ones((CHUNK, CHUNK), bool), -1)
    decay = jnp.exp(jnp.where(incl, gcum[..., :, None] - gcum[..., None, :], -jnp.inf))
    kb = kc * bc[..., None]
    a_mat = jnp.where(strict, jnp.einsum("bhnid,bhnjd->bhnij", kb, kc) * decay, 0.0)
    lhs = a_mat + jnp.eye(CHUNK, dtype=F32)
    rhs = jnp.concatenate([vc * bc[..., None], kb * jnp.exp(gcum)[..., None]], axis=-1)
    sol = lax.linalg.triangular_solve(lhs, rhs, left_side=True, lower=True)
    u, w = sol[..., :GDN_DV], sol[..., GDN_DV:]
    qk = jnp.einsum("bhnid,bhnjd->bhnij", qc, kc) * decay
    q_dec = qc * jnp.exp(gcum)[..., None]
    k_dec = kc * jnp.exp(gcum[..., -1:] - gcum)[..., None]
    c_dec = jnp.exp(gcum[..., -1])

    def step(state, blk):
        u_b, w_b, qk_b, qd_b, kd_b, cd_b = blk
        v_new = u_b - jnp.einsum("bhlk,bhkv->bhlv", w_b, state)
        o_b = (jnp.einsum("bhlk,bhkv->bhlv", qd_b, state)
               + jnp.einsum("bhij,bhjv->bhiv", qk_b, v_new))
        state = state * cd_b[..., None, None] + jnp.einsum("bhlk,bhlv->bhkv", kd_b, v_new)
        return state, o_b

    xs = tuple(jnp.moveaxis(t, 2, 0) for t in (u, w, qk, q_dec, k_dec, c_dec))
    state, o = lax.scan(step, state0, xs)
    o = jnp.transpose(o, (1, 0, 3, 2, 4)).reshape(B, T, H, GDN_DV)
    return o, state


def _flip(t, flip):
    return t[:, ::-1] if flip else t


def _gdn_out(o, z, o_norm, w_o):
    B, T = o.shape[:2]
    o = _rmsnorm(o, o_norm).reshape(B, T, GDN_VAL) * jax.nn.silu(z.astype(F32))
    return o.astype(w_o.dtype) @ w_o


def _gdn_mixer(h, h_ctx, w_in, conv_w, a_log, dt_bias, o_norm, w_o, need_ctx_out):
    q, k, v, g, beta, z = _gdn_project(h, w_in, conv_w, a_log, dt_bias)
    qc, kc, vc, gc, bc, zc = _gdn_project(h_ctx, w_in, conv_w, a_log, dt_bias)
    zero = jnp.zeros((h.shape[0], GDN_V_HEADS, GDN_DK, GDN_DV), F32)
    o = 0.0
    o_c = 0.0
    for d in range(2):
        fl = d == 1
        oc_d, s_ctx = _gdn_chunked(_flip(qc, fl), _flip(kc, fl), _flip(vc, fl),
                                   _flip(gc[:, :, d], fl), _flip(bc[:, :, d], fl), zero)
        o_d, _ = _gdn_chunked(_flip(q, fl), _flip(k, fl), _flip(v, fl),
                              _flip(g[:, :, d], fl), _flip(beta[:, :, d], fl), s_ctx)
        o = o + _flip(o_d, fl)
        o_c = o_c + _flip(oc_d, fl)
    y = _gdn_out(o, z, o_norm, w_o)
    y_ctx = _gdn_out(o_c, zc, o_norm, w_o) if need_ctx_out else None
    return y, y_ctx


def setup_inputs(seed: int = 0) -> dict:
    key = jax.random.key(seed)
    ks = jax.random.split(key, 20)
    D = D_MODEL

    def nrm(k, shape, scale):
        return jax.random.normal(k, shape, F32) * scale

    dt = jnp.exp(jax.random.uniform(ks[16], (N_GDN, 2, GDN_V_HEADS), F32,
                                    math.log(1e-3), math.log(1e-1)))
    return {
        "x": nrm(ks[0], (BATCH, SEQ, D), 1.0),
        "c": nrm(ks[1], (BATCH, D), 1.0),
        "ctx": nrm(ks[2], (BATCH, CTX_LEN, D), 1.0),
        "c_ctx": nrm(ks[3], (D,), 1.0),
        "ada_w": nrm(ks[4], (DEPTH, D, 3 * D), 0.5 * D ** -0.5),
        "ada_b": nrm(ks[5], (DEPTH, 3 * D), 0.02),
        "norm_w": 1.0 + nrm(ks[6], (DEPTH, D), 0.02),
        "mla_w_in": nrm(ks[7], (N_MLA, D, MLA_IN), D ** -0.5),
        "mla_q_norm": 1.0 + nrm(ks[8], (N_MLA, Q_LORA), 0.02),
        "mla_w_uq": nrm(ks[9], (N_MLA, Q_LORA, MLA_HEADS * MLA_QK), Q_LORA ** -0.5),
        "mla_kv_norm": 1.0 + nrm(ks[10], (N_MLA, KV_LORA), 0.02),
        "mla_w_ukv": nrm(ks[11], (N_MLA, KV_LORA, MLA_HEADS * (MLA_NOPE + MLA_V)), KV_LORA ** -0.5),
        "mla_w_o": nrm(ks[12], (N_MLA, MLA_WIDTH, D), MLA_WIDTH ** -0.5),
        "gdn_w_in": nrm(ks[13], (N_GDN, D, GDN_IN), D ** -0.5),
        "gdn_conv_w": nrm(ks[14], (N_GDN, CONV_W, GDN_CONV_CH), CONV_W ** -0.5),
        "gdn_a_log": jnp.log(jax.random.uniform(ks[15], (N_GDN, 2, GDN_V_HEADS), F32, 1.0, 16.0)),
        "gdn_dt_bias": dt + jnp.log(-jnp.expm1(-dt)),
        "gdn_o_norm": 1.0 + nrm(ks[17], (N_GDN, GDN_DV), 0.02),
        "gdn_w_o": nrm(ks[18], (N_GDN, GDN_VAL, D), GDN_VAL ** -0.5),
        "final_norm": 1.0 + nrm(ks[19], (D,), 0.02),
    }


def reference(x, c, ctx, c_ctx, ada_w, ada_b, norm_w, mla_w_in, mla_q_norm, mla_w_uq,
              mla_kv_norm, mla_w_ukv, mla_w_o, gdn_w_in, gdn_conv_w, gdn_a_log,
              gdn_dt_bias, gdn_o_norm, gdn_w_o, final_norm):
    cos, sin = _axial_rope_tables(x.shape[1])
    h_ctx = ctx
    for i in range(DEPTH):
        need_ctx_out = i < DEPTH - 1
        sh, sc, gt = _adaln(c, ada_w[i], ada_b[i])
        shc, scc, gtc = _adaln(c_ctx, ada_w[i], ada_b[i])
        a_in = _rmsnorm(x, norm_w[i]) * (1.0 + sc[:, None]) + sh[:, None]
        c_in = _rmsnorm(h_ctx, norm_w[i]) * (1.0 + scc) + shc
        j = i // N_MIXERS
        if i % N_MIXERS == 0:
            y, y_ctx = _mla_mixer(a_in, c_in, cos, sin, mla_w_in[j], mla_q_norm[j], mla_w_uq[j],
                                  mla_kv_norm[j], mla_w_ukv[j], mla_w_o[j], need_ctx_out)
        else:
            y, y_ctx = _gdn_mixer(a_in, c_in, gdn_w_in[j], gdn_conv_w[j], gdn_a_log[j],
                                  gdn_dt_bias[j], gdn_o_norm[j], gdn_w_o[j], need_ctx_out)
        x = x + gt[:, None] * y.astype(x.dtype)
        if need_ctx_out:
            h_ctx = h_ctx + gtc * y_ctx.astype(h_ctx.dtype)
    return _rmsnorm(x, final_norm)
```

```python
import functools

import jax
import jax.numpy as jnp
import numpy as np
from jax import lax
from jax.experimental import pallas as pl
from jax.experimental.pallas import tpu as pltpu

F32 = jnp.float32
MXU_DTYPE = jnp.bfloat16

D_MODEL = 2048
DEPTH = 4
EPS = 1e-6
GRID_W = 64
ROPE_THETA = 10000.0
MLA_HEADS = 16
MLA_NOPE = 128
MLA_ROPE = 64
MLA_V = 128
MLA_QK = MLA_NOPE + MLA_ROPE
Q_LORA = 512
KV_LORA = 512
MLA_WIDTH = MLA_HEADS * MLA_V
MLA_HEAD_COLS = 256
MLA_IN_PAD = Q_LORA + KV_LORA + MLA_WIDTH + 2 * MLA_ROPE
GDN_K_HEADS = 16
GDN_V_HEADS = 32
GDN_DK = 128
GDN_DV = 128
GDN_KEY = GDN_K_HEADS * GDN_DK
GDN_VAL = GDN_V_HEADS * GDN_DV
GDN_CONV_CH = 2 * GDN_KEY + GDN_VAL
GDN_QKVZ = GDN_CONV_CH + GDN_VAL
CONV_W = 5

LANES = 128
SUBLANES = 8
ROW_TILE = 256
MM_TM = 512
VMEM_LIMIT_BYTES = 50 * 1024 * 1024


def _params(*sem):
    return pltpu.CompilerParams(dimension_semantics=sem, vmem_limit_bytes=VMEM_LIMIT_BYTES)


def _sigmoid(x):
    return 1.0 / (1.0 + jnp.exp(-x))


def _silu(x):
    return x * _sigmoid(x)


def _mod_row(tile, n_ctx_tiles, tiles_per_batch, batch):
    return jnp.where(tile % tiles_per_batch < n_ctx_tiles, batch, tile // tiles_per_batch)


def _adaln_kernel(c_ref, w_ref, b_ref, o_ref):
    a = _silu(c_ref[...]).astype(MXU_DTYPE)
    w = w_ref[0].astype(MXU_DTYPE)
    o_ref[0] = jnp.dot(a, w, preferred_element_type=F32) + b_ref[0]


def _adaln(conds, ada_w, ada_b):
    depth, d, n = ada_w.shape
    tn = 768
    return pl.pallas_call(
        _adaln_kernel,
        grid=(depth, n // tn),
        in_specs=[pl.BlockSpec((SUBLANES, d), lambda l, j: (0, 0)),
                  pl.BlockSpec((1, d, tn), lambda l, j: (l, 0, j)),
                  pl.BlockSpec((1, 1, tn), lambda l, j: (l, 0, j))],
        out_specs=pl.BlockSpec((1, SUBLANES, tn), lambda l, j: (l, 0, j)),
        out_shape=jax.ShapeDtypeStruct((depth, SUBLANES, n), F32),
        compiler_params=_params("arbitrary", "arbitrary"),
        name="adaln",
    )(conds, ada_w, ada_b.reshape(depth, 1, n))


def _prenorm_kernel(x_ref, nw_ref, mod_ref, o_ref, *, n_ctx_tiles, tiles_per_batch, batch):
    row = _mod_row(pl.program_id(0), n_ctx_tiles, tiles_per_batch, batch)
    m = mod_ref[pl.ds(row, 1), :]
    x = x_ref[...]
    y = x * lax.rsqrt(jnp.mean(x * x, axis=-1, keepdims=True) + EPS) * nw_ref[...]
    o_ref[...] = (y * (1.0 + m[:, D_MODEL:2 * D_MODEL]) + m[:, :D_MODEL]).astype(o_ref.dtype)


def _prenorm(h, norm_w, mod, geom):
    t, d = h.shape
    return pl.pallas_call(
        functools.partial(_prenorm_kernel, **geom),
        grid=(t // ROW_TILE,),
        in_specs=[pl.BlockSpec((ROW_TILE, d), lambda i: (i, 0)),
                  pl.BlockSpec((1, d), lambda i: (0, 0)),
                  pl.BlockSpec((SUBLANES, 3 * d), lambda i: (0, 0))],
        out_specs=pl.BlockSpec((ROW_TILE, d), lambda i: (i, 0)),
        out_shape=jax.ShapeDtypeStruct((t, d), MXU_DTYPE),
        compiler_params=_params("arbitrary"),
        name="prenorm",
    )(h, norm_w.reshape(1, d), mod)


def _mm_kernel(a_ref, w_ref, o_ref, w_mxu_ref):
    @pl.when(pl.program_id(1) == 0)
    def _():
        w_mxu_ref[...] = w_ref[...].astype(MXU_DTYPE)

    o_ref[...] = jnp.dot(a_ref[...], w_mxu_ref[...], preferred_element_type=F32).astype(o_ref.dtype)


def _matmul(a, w, *, tn, first_block=0, n_blocks=None, out_dtype=F32, name="matmul"):
    m, k = a.shape
    n_blocks = w.shape[1] // tn if n_blocks is None else n_blocks
    return pl.pallas_call(
        _mm_kernel,
        grid=(n_blocks, m // MM_TM),
        in_specs=[pl.BlockSpec((MM_TM, k), lambda j, i: (i, 0)),
                  pl.BlockSpec((k, tn), lambda j, i: (0, first_block + j))],
        out_specs=pl.BlockSpec((MM_TM, tn), lambda j, i: (i, j)),
        out_shape=jax.ShapeDtypeStruct((m, n_blocks * tn), out_dtype),
        scratch_shapes=[pltpu.VMEM((k, tn), MXU_DTYPE)],
        compiler_params=_params("arbitrary", "arbitrary"),
        name=name,
    )(a, w)


def _mm_res_kernel(a_ref, w_ref, res_ref, gate_ref, o_ref, w_mxu_ref, *, n_ctx_tiles, tiles_per_batch, batch):
    @pl.when(pl.program_id(1) == 0)
    def _():
        w_mxu_ref[...] = w_ref[...].astype(MXU_DTYPE)

    acc = jnp.dot(a_ref[...], w_mxu_ref[...], preferred_element_type=F32)
    for s in range(MM_TM // ROW_TILE):
        tile = pl.program_id(1) * (MM_TM // ROW_TILE) + s
        g = gate_ref[pl.ds(_mod_row(tile, n_ctx_tiles, tiles_per_batch, batch), 1), :]
        rows = slice(s * ROW_TILE, (s + 1) * ROW_TILE)
        o_ref[rows, :] = res_ref[rows, :] + g * acc[rows, :]


def _matmul_residual(a, w, res, mod, geom, *, tn=512, name="matmul_res"):
    m, k = a.shape
    n = w.shape[1]
    gate_block0 = 2 * D_MODEL // tn
    return pl.pallas_call(
        functools.partial(_mm_res_kernel, **geom),
        grid=(n // tn, m // MM_TM),
        in_specs=[pl.BlockSpec((MM_TM, k), lambda j, i: (i, 0)),
                  pl.BlockSpec((k, tn), lambda j, i: (0, j)),
                  pl.BlockSpec((MM_TM, tn), lambda j, i: (i, j)),
                  pl.BlockSpec((SUBLANES, tn), lambda j, i: (0, gate_block0 + j))],
        out_specs=pl.BlockSpec((MM_TM, tn), lambda j, i: (i, j)),
        out_shape=jax.ShapeDtypeStruct((m, n), F32),
        scratch_shapes=[pltpu.VMEM((k, tn), MXU_DTYPE)],
        compiler_params=_params("arbitrary", "arbitrary"),
        name=name,
    )(a, w, res, mod)


def _mla_up_kernel(cq_ref, ckv_ref, kr_ref, qn_ref, kvn_ref, wq_ref, wkv_ref, tab_ref,
                   q_ref, kv_ref, kro_ref):
    def rms(x, w):
        return (x * lax.rsqrt(jnp.mean(x * x, axis=-1, keepdims=True) + EPS) * w).astype(MXU_DTYPE)

    def rotate(pair):
        t = pair * tab_ref[...]
        return t + pltpu.roll(t, MLA_ROPE, axis=1)

    cq = rms(cq_ref[...], qn_ref[...])
    ckv = rms(ckv_ref[...], kvn_ref[...])
    for h in range(MLA_HEADS):
        cols = slice(h * MLA_HEAD_COLS, (h + 1) * MLA_HEAD_COLS)
        qh = jnp.dot(cq, wq_ref[:, cols], preferred_element_type=F32) * (MLA_QK ** -0.5)
        q_ref[:, h * MLA_HEAD_COLS:h * MLA_HEAD_COLS + MLA_NOPE] = qh[:, :MLA_NOPE].astype(q_ref.dtype)
        q_ref[:, h * MLA_HEAD_COLS + MLA_NOPE:(h + 1) * MLA_HEAD_COLS] = rotate(qh[:, MLA_NOPE:]).astype(q_ref.dtype)
        kv_ref[:, cols] = jnp.dot(ckv, wkv_ref[:, cols], preferred_element_type=F32).astype(kv_ref.dtype)
    lane = lax.broadcasted_iota(jnp.int32, kro_ref.shape, 1)
    kro_ref[...] = jnp.where(lane < MLA_ROPE, rotate(kr_ref[...]), 0.0).astype(kro_ref.dtype)


def _mla_up(proj, q_norm, kv_norm, wq, wkv, tab):
    t = proj.shape[0]
    tm = ROW_TILE
    nq = MLA_HEADS * MLA_HEAD_COLS
    nkv = MLA_HEADS * (MLA_NOPE + MLA_V)
    kr_block = (Q_LORA + KV_LORA + MLA_WIDTH) // LANES
    return pl.pallas_call(
        _mla_up_kernel,
        grid=(t // tm,),
        in_specs=[pl.BlockSpec((tm, Q_LORA), lambda i: (i, 0)),
                  pl.BlockSpec((tm, KV_LORA), lambda i: (i, 1)),
                  pl.BlockSpec((tm, LANES), lambda i: (i, kr_block)),
                  pl.BlockSpec((1, Q_LORA), lambda i: (0, 0)),
                  pl.BlockSpec((1, KV_LORA), lambda i: (0, 0)),
                  pl.BlockSpec((Q_LORA, nq), lambda i: (0, 0)),
                  pl.BlockSpec((KV_LORA, nkv), lambda i: (0, 0)),
                  pl.BlockSpec((tm, LANES), lambda i: (i, 0))],
        out_specs=[pl.BlockSpec((tm, nq), lambda i: (i, 0)),
                   pl.BlockSpec((tm, nkv), lambda i: (i, 0)),
                   pl.BlockSpec((tm, LANES), lambda i: (i, 0))],
        out_shape=[jax.ShapeDtypeStruct((t, nq), MXU_DTYPE),
                   jax.ShapeDtypeStruct((t, nkv), MXU_DTYPE),
                   jax.ShapeDtypeStruct((t, LANES), MXU_DTYPE)],
        compiler_params=_params("arbitrary"),
        name="mla_up",
    )(proj, proj, proj, q_norm.reshape(1, -1), kv_norm.reshape(1, -1), wq, wkv, tab)


def _attn_kernel(q_ref, k_ref, v_ref, kr_ref, z_ref, o_ref, kfull_ref, *, n_ctx_tiles, tiles_per_batch, tk):
    i = pl.program_id(2)

    @pl.when(i == 0)
    def _():
        kfull_ref[:, :MLA_NOPE] = k_ref[...]
        kfull_ref[:, MLA_NOPE:] = kr_ref[...]

    n_kv = jnp.where(i < n_ctx_tiles, n_ctx_tiles, tiles_per_batch) * (ROW_TILE // tk)
    q = q_ref[...]

    def body(j, carry):
        m, l, acc = carry
        off = pl.multiple_of(j * tk, tk)
        s = lax.dot_general(q, kfull_ref[pl.ds(off, tk), :], (((1,), (1,)), ((), ())),
                            preferred_element_type=F32)
        m_new = jnp.maximum(m, jnp.max(s, axis=-1, keepdims=True))
        alpha = jnp.exp(m - m_new)
        p = jnp.exp(s - m_new)
        l = alpha * l + jnp.sum(p, axis=-1, keepdims=True)
        acc = alpha * acc + jnp.dot(p.astype(MXU_DTYPE), v_ref[pl.ds(off, tk), :],
                                    preferred_element_type=F32)
        return m_new, l, acc

    init = (jnp.full((ROW_TILE, 1), -jnp.inf, F32), jnp.zeros((ROW_TILE, 1), F32),
            jnp.zeros((ROW_TILE, MLA_V), F32))
    _, l, acc = lax.fori_loop(0, n_kv, body, init)
    o_ref[...] = (acc / l * _silu(z_ref[...])).astype(o_ref.dtype)


def _attention(q, kv, kr, proj, geom, batch):
    t = q.shape[0]
    tpb = geom["tiles_per_batch"]
    p_rows = tpb * ROW_TILE
    z_block0 = (Q_LORA + KV_LORA) // LANES
    return pl.pallas_call(
        functools.partial(_attn_kernel, n_ctx_tiles=geom["n_ctx_tiles"], tiles_per_batch=tpb, tk=ROW_TILE),
        grid=(batch, MLA_HEADS, tpb),
        in_specs=[pl.BlockSpec((ROW_TILE, MLA_HEAD_COLS), lambda b, h, i: (b * tpb + i, h)),
                  pl.BlockSpec((p_rows, MLA_NOPE), lambda b, h, i: (b, 2 * h)),
                  pl.BlockSpec((p_rows, MLA_V), lambda b, h, i: (b, 2 * h + 1)),
                  pl.BlockSpec((p_rows, LANES), lambda b, h, i: (b, 0)),
                  pl.BlockSpec((ROW_TILE, MLA_V), lambda b, h, i: (b * tpb + i, z_block0 + h))],
        out_specs=pl.BlockSpec((ROW_TILE, MLA_V), lambda b, h, i: (b * tpb + i, h)),
        out_shape=jax.ShapeDtypeStruct((t, MLA_WIDTH), MXU_DTYPE),
        scratch_shapes=[pltpu.VMEM((p_rows, MLA_NOPE + LANES), MXU_DTYPE)],
        compiler_params=_params("arbitrary", "arbitrary", "arbitrary"),
        name="mla_attention",
    )(q, kv, kv, kr, proj)


def _gdn_conv_kernel(cur_ref, prev_ref, next_ref, w_ref, o_ref, pad_ref, *, n_ctx_tiles, tiles_per_batch, tc):
    t = pl.program_id(0)
    c = pl.program_id(1)
    p = t % tiles_per_batch
    has_prev = jnp.logical_and(p != 0, p != n_ctx_tiles)
    has_next = jnp.logical_and(p != n_ctx_tiles - 1, p != tiles_per_batch - 1)
    pad_ref[0:SUBLANES, :] = jnp.where(has_prev, prev_ref[...], 0.0)
    pad_ref[SUBLANES:SUBLANES + ROW_TILE, :] = cur_ref[...]
    pad_ref[SUBLANES + ROW_TILE:, :] = jnp.where(has_next, next_ref[...], 0.0)
    w = w_ref[...]
    y = jnp.zeros((ROW_TILE, tc), F32)
    for k in range(CONV_W):
        start = SUBLANES - CONV_W // 2 + k
        y = y + pad_ref[start:start + ROW_TILE, :] * w[k:k + 1, :]
    y = _silu(y)
    key_tiles = GDN_KEY // tc

    @pl.when(c < 2 * key_tiles)
    def _():
        scale = jnp.where(c < key_tiles, GDN_DK ** -0.5, 1.0)
        for g in range(tc // GDN_DK):
            yg = y[:, g * GDN_DK:(g + 1) * GDN_DK]
            inv = lax.rsqrt(jnp.sum(yg * yg, axis=-1, keepdims=True) + EPS) * scale
            o_ref[:, g * GDN_DK:(g + 1) * GDN_DK] = (yg * inv).astype(o_ref.dtype)

    @pl.when(c >= 2 * key_tiles)
    def _():
        o_ref[...] = y.astype(o_ref.dtype)


def _gdn_conv(proj, conv_w, geom):
    t = proj.shape[0]
    tc = 512
    halo_per_tile = ROW_TILE // SUBLANES
    last_halo = t // SUBLANES - 1
    return pl.pallas_call(
        functools.partial(_gdn_conv_kernel, n_ctx_tiles=geom["n_ctx_tiles"],
                          tiles_per_batch=geom["tiles_per_batch"], tc=tc),
        grid=(t // ROW_TILE, GDN_CONV_CH // tc),
        in_specs=[pl.BlockSpec((ROW_TILE, tc), lambda i, c: (i, c)),
                  pl.BlockSpec((SUBLANES, tc), lambda i, c: (jnp.maximum(i * halo_per_tile - 1, 0), c)),
                  pl.BlockSpec((SUBLANES, tc), lambda i, c: (jnp.minimum((i + 1) * halo_per_tile, last_halo), c)),
                  pl.BlockSpec((CONV_W, tc), lambda i, c: (0, c))],
        out_specs=pl.BlockSpec((ROW_TILE, tc), lambda i, c: (i, c)),
        out_shape=jax.ShapeDtypeStruct((t, GDN_CONV_CH), MXU_DTYPE),
        scratch_shapes=[pltpu.VMEM((ROW_TILE + 2 * SUBLANES, tc), F32)],
        compiler_params=_params("arbitrary", "arbitrary"),
        name="gdn_conv",
    )(proj, proj, proj, conv_w)


def _split3(x):
    hi = x.astype(MXU_DTYPE)
    r = x - hi.astype(F32)
    mid = r.astype(MXU_DTYPE)
    lo = (r - mid.astype(F32)).astype(MXU_DTYPE)
    return hi, mid, lo


def _gdn_gates_kernel(ba_ref, alog_ref, dtb_ref, gb_ref, gbt_ref):
    half = LANES // 2
    ba = ba_ref[...]
    beta = _sigmoid(ba[:, :half])
    x = ba[:, half:] + dtb_ref[...]
    softplus = jnp.maximum(x, 0.0) + jnp.log(1.0 + jnp.exp(-jnp.abs(x)))
    g = -jnp.exp(alog_ref[...]) * softplus
    row = lax.broadcasted_iota(jnp.int32, (ROW_TILE, ROW_TILE), 0)
    col = lax.broadcasted_iota(jnp.int32, (ROW_TILE, ROW_TILE), 1)
    lower = (row >= col).astype(MXU_DTYPE)
    upper = (row <= col).astype(MXU_DTYPE)
    pre = jnp.zeros((ROW_TILE, half), F32)
    suf = jnp.zeros((ROW_TILE, half), F32)
    for piece in _split3(g):
        pre = pre + jnp.dot(lower, piece, preferred_element_type=F32)
        suf = suf + jnp.dot(upper, piece, preferred_element_type=F32)
    lane = lax.broadcasted_iota(jnp.int32, (ROW_TILE, half), 1)
    gcum = jnp.where(lane < half // 2, pre, suf)
    gb = jnp.concatenate([beta, gcum], axis=-1)
    gb_ref[...] = gb
    gbt_ref[...] = gb.T


def _gdn_gates(ba, a_log, dt_bias):
    t = ba.shape[0]
    half = LANES // 2
    return pl.pallas_call(
        _gdn_gates_kernel,
        grid=(t // ROW_TILE,),
        in_specs=[pl.BlockSpec((ROW_TILE, LANES), lambda i: (i, 0)),
                  pl.BlockSpec((1, half), lambda i: (0, 0)),
                  pl.BlockSpec((1, half), lambda i: (0, 0))],
        out_specs=[pl.BlockSpec((ROW_TILE, LANES), lambda i: (i, 0)),
                   pl.BlockSpec((LANES, ROW_TILE), lambda i: (0, i))],
        out_shape=[jax.ShapeDtypeStruct((t, LANES), F32), jax.ShapeDtypeStruct((LANES, t), F32)],
        compiler_params=_params("arbitrary"),
        name="gdn_gates",
    )(ba, a_log.reshape(1, half), dt_bias.reshape(1, half))


def _mxu_dot(a, b):
    return jnp.dot(a.astype(MXU_DTYPE), b.astype(MXU_DTYPE), preferred_element_type=F32)


def _gdn_scan_kernel(q_ref, k_ref, v_ref, gb_ref, gbt_ref, o_ref, state_ref):
    n = ROW_TILE
    hk = pl.program_id(1)
    d = pl.program_id(2)

    @pl.when(pl.program_id(3) == 0)
    def _():
        state_ref[...] = jnp.zeros(state_ref.shape, F32)

    q = q_ref[...]
    k = k_ref[...]
    gram = lax.dot_general(jnp.concatenate([q, k], axis=0), k, (((1,), (1,)), ((), ())),
                           preferred_element_type=F32)
    qk, kk = gram[:n], gram[n:]
    row = lax.broadcasted_iota(jnp.int32, (n, n), 0)
    col = lax.broadcasted_iota(jnp.int32, (n, n), 1)
    fwd = d == 0
    ahead = (row - col) * jnp.where(fwd, 1, -1)
    incl = ahead >= 0
    strict = ahead > 0
    eye = (row == col).astype(F32)
    last = jnp.where(fwd, n - 1, 0)
    lane = lax.broadcasted_iota(jnp.int32, (n, LANES), 1)
    sub1 = lax.broadcasted_iota(jnp.int32, (n, 1), 0)
    gb = gb_ref[...]
    half = LANES // 2
    for e in range(GDN_V_HEADS // GDN_K_HEADS):
        c_beta = d * GDN_V_HEADS + hk * (GDN_V_HEADS // GDN_K_HEADS) + e
        c_g = half + c_beta
        beta_c = jnp.sum(jnp.where(lane == c_beta, gb, 0.0), axis=-1, keepdims=True)
        g_c = jnp.sum(jnp.where(lane == c_g, gb, 0.0), axis=-1, keepdims=True)
        beta_r = gbt_ref[pl.ds(c_beta, 1), :]
        g_r = gbt_ref[pl.ds(c_g, 1), :]
        g_last = jnp.sum(jnp.where(sub1 == last, g_c, 0.0), axis=0, keepdims=True)
        decay = jnp.exp(jnp.where(incl, g_c - g_r, -jnp.inf))
        a_mat = jnp.where(strict, beta_c * kk * decay, 0.0)
        block_xor = row ^ col
        inv = eye - jnp.where(block_xor == 1, a_mat, 0.0)
        for log_m in range(1, int(np.log2(n))):
            a_off = jnp.where((block_xor >> log_m) == 1, a_mat, 0.0)
            inv = inv - _mxu_dot(inv, _mxu_dot(a_off, inv))
        eg_r = jnp.exp(g_r)
        u = _mxu_dot(inv * beta_r, v_ref[:, e * GDN_DV:(e + 1) * GDN_DV])
        w = _mxu_dot(inv * (beta_r * eg_r), k)
        state = state_ref[e]
        v_new = u - _mxu_dot(w, state)
        q_dec = q.astype(F32) * jnp.exp(g_c)
        k_dec = k.astype(F32) * jnp.exp(g_last - g_c)
        o = _mxu_dot(q_dec, state) + _mxu_dot(qk * decay, v_new)
        o_ref[0, :, e * GDN_DV:(e + 1) * GDN_DV] = o
        state_ref[e] = state * jnp.exp(g_last) + lax.dot_general(
            k_dec.astype(MXU_DTYPE), v_new.astype(MXU_DTYPE), (((0,), (0,)), ((), ())),
            preferred_element_type=F32)


def _gdn_scan(qkv, gb, gbt, geom, batch):
    t = qkv.shape[0]
    n_ctx, tpb = geom["n_ctx_tiles"], geom["tiles_per_batch"]
    rep = GDN_V_HEADS // GDN_K_HEADS

    def tile(b, d, i):
        bwd = jnp.where(i < n_ctx, n_ctx - 1 - i, tpb - 1 - i + n_ctx)
        return b * tpb + jnp.where(d == 0, i, bwd)

    return pl.pallas_call(
        _gdn_scan_kernel,
        grid=(batch, GDN_K_HEADS, 2, tpb),
        in_specs=[pl.BlockSpec((ROW_TILE, GDN_DK), lambda b, h, d, i: (tile(b, d, i), h)),
                  pl.BlockSpec((ROW_TILE, GDN_DK), lambda b, h, d, i: (tile(b, d, i), GDN_K_HEADS + h)),
                  pl.BlockSpec((ROW_TILE, rep * GDN_DV), lambda b, h, d, i: (tile(b, d, i), GDN_K_HEADS + h)),
                  pl.BlockSpec((ROW_TILE, LANES), lambda b, h, d, i: (tile(b, d, i), 0)),
                  pl.BlockSpec((LANES, ROW_TILE), lambda b, h, d, i: (0, tile(b, d, i)))],
        out_specs=pl.BlockSpec((1, ROW_TILE, rep * GDN_DV), lambda b, h, d, i: (d, tile(b, d, i), h)),
        out_shape=jax.ShapeDtypeStruct((2, t, GDN_VAL), F32),
        scratch_shapes=[pltpu.VMEM((rep, GDN_DK, GDN_DV), F32)],
        compiler_params=_params("arbitrary", "arbitrary", "arbitrary", "arbitrary"),
        name="gdn_scan",
    )(qkv, qkv, qkv, gb, gbt)


def _gdn_gate_kernel(o_ref, z_ref, nw_ref, a_ref):
    o = o_ref[0] + o_ref[1]
    for g in range(o.shape[1] // GDN_DV):
        cols = slice(g * GDN_DV, (g + 1) * GDN_DV)
        og = o[:, cols]
        y = og * lax.rsqrt(jnp.mean(og * og, axis=-1, keepdims=True) + EPS) * nw_ref[...]
        a_ref[:, cols] = (y * _silu(z_ref[:, cols])).astype(a_ref.dtype)


def _gdn_gate(o, proj, o_norm):
    t = o.shape[1]
    tc = 512
    z_block0 = GDN_CONV_CH // tc
    return pl.pallas_call(
        _gdn_gate_kernel,
        grid=(t // ROW_TILE, GDN_VAL // tc),
        in_specs=[pl.BlockSpec((2, ROW_TILE, tc), lambda i, c: (0, i, c)),
                  pl.BlockSpec((ROW_TILE, tc), lambda i, c: (i, z_block0 + c)),
                  pl.BlockSpec((1, GDN_DV), lambda i, c: (0, 0))],
        out_specs=pl.BlockSpec((ROW_TILE, tc), lambda i, c: (i, c)),
        out_shape=jax.ShapeDtypeStruct((t, GDN_VAL), MXU_DTYPE),
        compiler_params=_params("arbitrary", "arbitrary"),
        name="gdn_gate",
    )(o, proj, o_norm.reshape(1, GDN_DV))


def _final_norm_kernel(x_ref, w_ref, o_ref):
    x = x_ref[...]
    o_ref[...] = x * lax.rsqrt(jnp.mean(x * x, axis=-1, keepdims=True) + EPS) * w_ref[...]


def _final_norm(h, w, geom, batch):
    d = h.shape[1]
    n_ctx, tpb = geom["n_ctx_tiles"], geom["tiles_per_batch"]
    n_lat = tpb - n_ctx
    return pl.pallas_call(
        _final_norm_kernel,
        grid=(batch * n_lat,),
        in_specs=[pl.BlockSpec((ROW_TILE, d), lambda i: ((i // n_lat) * tpb + n_ctx + i % n_lat, 0)),
                  pl.BlockSpec((1, d), lambda i: (0, 0))],
        out_specs=pl.BlockSpec((ROW_TILE, d), lambda i: (i, 0)),
        out_shape=jax.ShapeDtypeStruct((batch * n_lat * ROW_TILE, d), F32),
        compiler_params=_params("arbitrary"),
        name="final_norm",
    )(h, w.reshape(1, d))


def _rope_table(batch, ctx_len, seq):
    pos = jnp.arange(seq)
    r = (pos // GRID_W).astype(F32)
    col = (pos % GRID_W).astype(F32)
    half = MLA_ROPE // 2
    inv = ROPE_THETA ** (-jnp.arange(0, half, 2, dtype=F32) / half)
    ang = jnp.concatenate([r[:, None] * inv, col[:, None] * inv], axis=-1)
    cos, sin = jnp.cos(ang), jnp.sin(ang)
    lat = jnp.concatenate([cos, cos, -sin, sin], axis=-1)
    ident = jnp.concatenate([jnp.ones((ctx_len, MLA_ROPE), F32), jnp.zeros((ctx_len, MLA_ROPE), F32)], axis=-1)
    return jnp.tile(jnp.concatenate([ident, lat], axis=0), (batch, 1))


_ROPE_SWAP = np.concatenate([np.arange(MLA_ROPE // 2, MLA_ROPE), np.arange(MLA_ROPE // 2)])


def _mla_weights(w_in, w_uq, w_ukv):
    lo = Q_LORA + KV_LORA
    k_rope = w_in[:, lo:lo + MLA_ROPE]
    w_in_p = jnp.concatenate([w_in[:, :lo], w_in[:, lo + MLA_ROPE:], k_rope, k_rope[:, _ROPE_SWAP]], axis=1)
    head = np.concatenate([np.arange(MLA_QK), MLA_NOPE + _ROPE_SWAP])
    cols = (np.arange(MLA_HEADS)[:, None] * MLA_QK + head[None, :]).reshape(-1)
    return w_in_p, w_uq[:, cols].astype(MXU_DTYPE), w_ukv.astype(MXU_DTYPE)


def _mla_layer(h, a, mod, tab, w_in, q_norm, w_uq, kv_norm, w_ukv, w_o, geom, batch):
    w_in_p, wq, wkv = _mla_weights(w_in, w_uq, w_ukv)
    proj = _matmul(a, w_in_p, tn=640, name="mla_in")
    q, kv, kr = _mla_up(proj, q_norm, kv_norm, wq, wkv, tab)
    gated = _attention(q, kv, kr, proj, geom, batch)
    return _matmul_residual(gated, w_o, h, mod, geom, name="mla_out")


def _gdn_layer(h, a, mod, w_in, conv_w, a_log, dt_bias, o_norm, w_o, geom, batch):
    proj = _matmul(a, w_in, tn=1024, n_blocks=GDN_QKVZ // 1024, name="gdn_in")
    ba = _matmul(a, w_in, tn=LANES, first_block=GDN_QKVZ // LANES, n_blocks=1, name="gdn_in_gates")
    qkv = _gdn_conv(proj, conv_w, geom)
    gb, gbt = _gdn_gates(ba, a_log, dt_bias)
    o = _gdn_scan(qkv, gb, gbt, geom, batch)
    gated = _gdn_gate(o, proj, o_norm)
    return _matmul_residual(gated, w_o, h, mod, geom, name="gdn_out")


def kernel(x, c, ctx, c_ctx, ada_w, ada_b, norm_w, mla_w_in, mla_q_norm, mla_w_uq, mla_kv_norm, mla_w_ukv, mla_w_o, gdn_w_in, gdn_conv_w, gdn_a_log, gdn_dt_bias, gdn_o_norm, gdn_w_o, final_norm):
    batch, seq, d = x.shape
    ctx_len = ctx.shape[1]
    assert d == D_MODEL and seq % ROW_TILE == 0 and ctx_len % ROW_TILE == 0 and seq % GRID_W == 0
    assert batch < SUBLANES and (batch * (ctx_len + seq)) % MM_TM == 0
    geom = dict(n_ctx_tiles=ctx_len // ROW_TILE, tiles_per_batch=(ctx_len + seq) // ROW_TILE)
    mod_geom = dict(geom, batch=batch)

    h = jnp.concatenate([ctx, x], axis=1).reshape(batch * (ctx_len + seq), d)
    conds = jnp.zeros((SUBLANES, d), F32).at[:batch].set(c).at[batch].set(c_ctx)
    mods = _adaln(conds, ada_w, ada_b)
    tab = _rope_table(batch, ctx_len, seq)

    for i in range(DEPTH):
        a = _prenorm(h, norm_w[i], mods[i], mod_geom)
        j = i // 2
        if i % 2 == 0:
            h = _mla_layer(h, a, mods[i], tab, mla_w_in[j], mla_q_norm[j], mla_w_uq[j], mla_kv_norm[j],
                           mla_w_ukv[j], mla_w_o[j], mod_geom, batch)
        else:
            h = _gdn_layer(h, a, mods[i], gdn_w_in[j], gdn_conv_w[j], gdn_a_log[j], gdn_dt_bias[j],
                           gdn_o_norm[j], gdn_w_o[j], mod_geom, batch)
    out = _final_norm(h, final_norm, geom, batch)
    return out.reshape(batch, seq, d)
```

```python
import functools
import math
from typing import NamedTuple

import jax
import jax.numpy as jnp
import numpy as np
from jax import lax
from jax.experimental import pallas as pl
from jax.experimental.pallas import tpu as pltpu

F32 = jnp.float32
MXU_DTYPE = jnp.bfloat16

D_MODEL = 2048
DEPTH = 4
EPS = 1e-6
GRID_W = 64
ROPE_THETA = 10000.0
MLA_HEADS = 16
MLA_NOPE = 128
MLA_ROPE = 64
MLA_V = 128
MLA_QK = MLA_NOPE + MLA_ROPE
Q_LORA = 512
KV_LORA = 512
MLA_WIDTH = MLA_HEADS * MLA_V
MLA_HEAD_COLS = 256
MLA_IN_PAD = Q_LORA + KV_LORA + MLA_WIDTH + 2 * MLA_ROPE
GDN_K_HEADS = 16
GDN_V_HEADS = 32
GDN_DK = 128
GDN_DV = 128
GDN_KEY = GDN_K_HEADS * GDN_DK
GDN_VAL = GDN_V_HEADS * GDN_DV
GDN_CONV_CH = 2 * GDN_KEY + GDN_VAL
GDN_QKVZ = GDN_CONV_CH + GDN_VAL
CONV_W = 5

LANES = 128
SUBLANES = 8
ROW_TILE = 256
MM_TM = 512
ATTN_TQ = 512
ATTN_TK = 256
VMEM_LIMIT_BYTES = 50 * 1024 * 1024


class Geom(NamedTuple):
    batch: int
    n_lat: int
    n_ctx: int

    @property
    def lat_tiles(self):
        return self.batch * self.n_lat

    def mod_row(self, tile):
        return jnp.where(tile < self.lat_tiles, tile // self.n_lat, self.batch)

    def seq_pos(self, tile):
        is_lat = tile < self.lat_tiles
        pos = jnp.where(is_lat, tile % self.n_lat, (tile - self.lat_tiles) % self.n_ctx)
        return pos, jnp.where(is_lat, self.n_lat, self.n_ctx)


def _params(*sem):
    return pltpu.CompilerParams(dimension_semantics=sem, vmem_limit_bytes=VMEM_LIMIT_BYTES)


def _sigmoid(x):
    return 1.0 / (1.0 + jnp.exp(-x))


def _silu(x):
    return x * _sigmoid(x)


def _nt_dot(a, b):
    return lax.dot_general(a, b, (((1,), (1,)), ((), ())), preferred_element_type=F32)


def _adaln_kernel(c_ref, w_ref, b_ref, o_ref):
    a = _silu(c_ref[...]).astype(MXU_DTYPE)
    w = w_ref[0].astype(MXU_DTYPE)
    o_ref[0] = jnp.dot(a, w, preferred_element_type=F32) + b_ref[0]


def _adaln(conds, ada_w, ada_b):
    depth, d, n = ada_w.shape
    tn = 768
    return pl.pallas_call(
        _adaln_kernel,
        grid=(depth, n // tn),
        in_specs=[pl.BlockSpec((SUBLANES, d), lambda l, j: (0, 0)),
                  pl.BlockSpec((1, d, tn), lambda l, j: (l, 0, j)),
                  pl.BlockSpec((1, 1, tn), lambda l, j: (l, 0, j))],
        out_specs=pl.BlockSpec((1, SUBLANES, tn), lambda l, j: (l, 0, j)),
        out_shape=jax.ShapeDtypeStruct((depth, SUBLANES, n), F32),
        compiler_params=_params("arbitrary", "arbitrary"),
        name="adaln",
    )(conds, ada_w, ada_b.reshape(depth, 1, n))


def _prenorm_kernel(x_ref, nw_ref, mod_ref, o_ref, *, geom):
    m = mod_ref[pl.ds(geom.mod_row(pl.program_id(0)), 1), :]
    x = x_ref[...]
    y = x * lax.rsqrt(jnp.mean(x * x, axis=-1, keepdims=True) + EPS) * nw_ref[...]
    o_ref[...] = (y * (1.0 + m[:, D_MODEL:2 * D_MODEL]) + m[:, :D_MODEL]).astype(o_ref.dtype)


def _prenorm(h, norm_w, mod, geom):
    t, d = h.shape
    return pl.pallas_call(
        functools.partial(_prenorm_kernel, geom=geom),
        grid=(t // ROW_TILE,),
        in_specs=[pl.BlockSpec((ROW_TILE, d), lambda i: (i, 0)),
                  pl.BlockSpec((1, d), lambda i: (0, 0)),
                  pl.BlockSpec((SUBLANES, 3 * d), lambda i: (0, 0))],
        out_specs=pl.BlockSpec((ROW_TILE, d), lambda i: (i, 0)),
        out_shape=jax.ShapeDtypeStruct((t, d), MXU_DTYPE),
        compiler_params=_params("arbitrary"),
        name="prenorm",
    )(h, norm_w.reshape(1, d), mod)


def _mm_kernel(a_ref, w_ref, o_ref, w_mxu_ref):
    @pl.when(pl.program_id(1) == 0)
    def _():
        w_mxu_ref[...] = w_ref[...].astype(MXU_DTYPE)

    o_ref[...] = jnp.dot(a_ref[...], w_mxu_ref[...], preferred_element_type=F32).astype(o_ref.dtype)


def _matmul(a, w, *, tn, first_block=0, n_blocks=None, out_dtype=F32, name="matmul"):
    m, k = a.shape
    n_blocks = w.shape[1] // tn if n_blocks is None else n_blocks
    return pl.pallas_call(
        _mm_kernel,
        grid=(n_blocks, m // MM_TM),
        in_specs=[pl.BlockSpec((MM_TM, k), lambda j, i: (i, 0)),
                  pl.BlockSpec((k, tn), lambda j, i: (0, first_block + j))],
        out_specs=pl.BlockSpec((MM_TM, tn), lambda j, i: (i, j)),
        out_shape=jax.ShapeDtypeStruct((m, n_blocks * tn), out_dtype),
        scratch_shapes=[pltpu.VMEM((k, tn), MXU_DTYPE)],
        compiler_params=_params("arbitrary", "arbitrary"),
        name=name,
    )(a, w)


def _mm_res_kernel(a_ref, w_ref, res_ref, gate_ref, o_ref, w_mxu_ref, *, geom):
    @pl.when(pl.program_id(1) == 0)
    def _():
        w_mxu_ref[...] = w_ref[...].astype(MXU_DTYPE)

    acc = jnp.dot(a_ref[...], w_mxu_ref[...], preferred_element_type=F32)
    for s in range(MM_TM // ROW_TILE):
        tile = pl.program_id(1) * (MM_TM // ROW_TILE) + s
        g = gate_ref[pl.ds(geom.mod_row(tile), 1), :]
        rows = slice(s * ROW_TILE, (s + 1) * ROW_TILE)
        o_ref[rows, :] = res_ref[rows, :] + g * acc[rows, :]


def _matmul_residual(a, w, res, mod, geom, *, tn=512, name="matmul_res"):
    m, k = a.shape
    n = w.shape[1]
    gate_block0 = 2 * D_MODEL // tn
    return pl.pallas_call(
        functools.partial(_mm_res_kernel, geom=geom),
        grid=(n // tn, m // MM_TM),
        in_specs=[pl.BlockSpec((MM_TM, k), lambda j, i: (i, 0)),
                  pl.BlockSpec((k, tn), lambda j, i: (0, j)),
                  pl.BlockSpec((MM_TM, tn), lambda j, i: (i, j)),
                  pl.BlockSpec((SUBLANES, tn), lambda j, i: (0, gate_block0 + j))],
        out_specs=pl.BlockSpec((MM_TM, tn), lambda j, i: (i, j)),
        out_shape=jax.ShapeDtypeStruct((m, n), F32),
        scratch_shapes=[pltpu.VMEM((k, tn), MXU_DTYPE)],
        compiler_params=_params("arbitrary", "arbitrary"),
        name=name,
    )(a, w, res, mod)


def _mla_up_kernel(cq_ref, ckv_ref, kr_ref, qn_ref, kvn_ref, wqt_ref, wk_ref, wvt_ref, tab_ref, tabt_ref,
                   qt_ref, k_ref, vt_ref):
    def rms(x, w):
        return (x * lax.rsqrt(jnp.mean(x * x, axis=-1, keepdims=True) + EPS) * w).astype(MXU_DTYPE)

    cq = rms(cq_ref[...], qn_ref[...])
    ckv = rms(ckv_ref[...], kvn_ref[...])
    tm = cq.shape[0]
    t = kr_ref[...] * tab_ref[...]
    lane = lax.broadcasted_iota(jnp.int32, t.shape, 1)
    k_rot = jnp.where(lane < MLA_ROPE, t + pltpu.roll(t, MLA_ROPE, axis=1), 0.0).astype(k_ref.dtype)
    tabt = tabt_ref[...]
    zeros = jnp.zeros((MLA_ROPE, tm), qt_ref.dtype)
    q_scale = MLA_QK ** -0.5 * math.log2(math.e)
    for h in range(MLA_HEADS):
        q0 = h * MLA_HEAD_COLS
        qh = _nt_dot(wqt_ref[q0:q0 + MLA_HEAD_COLS, :], cq) * q_scale
        qt_ref[q0:q0 + MLA_NOPE, :] = qh[:MLA_NOPE].astype(qt_ref.dtype)
        tq = qh[MLA_NOPE:] * tabt
        qt_ref[q0 + MLA_NOPE:q0 + MLA_NOPE + MLA_ROPE, :] = (tq[:MLA_ROPE] + tq[MLA_ROPE:]).astype(qt_ref.dtype)
        qt_ref[q0 + MLA_NOPE + MLA_ROPE:q0 + MLA_HEAD_COLS, :] = zeros
        k_ref[:, q0:q0 + MLA_NOPE] = jnp.dot(ckv, wk_ref[:, h * MLA_NOPE:(h + 1) * MLA_NOPE],
                                            preferred_element_type=F32).astype(k_ref.dtype)
        k_ref[:, q0 + MLA_NOPE:q0 + MLA_HEAD_COLS] = k_rot
        vt_ref[h * MLA_V:(h + 1) * MLA_V, :] = _nt_dot(wvt_ref[h * MLA_V:(h + 1) * MLA_V, :], ckv).astype(vt_ref.dtype)


def _mla_up(proj, q_norm, kv_norm, wqt, wk, wvt, tab, tabt):
    t = proj.shape[0]
    tm = ROW_TILE
    nq = MLA_HEADS * MLA_HEAD_COLS
    kr_block = (Q_LORA + KV_LORA + MLA_WIDTH) // LANES
    const = lambda i: (0, 0)
    return pl.pallas_call(
        _mla_up_kernel,
        grid=(t // tm,),
        in_specs=[pl.BlockSpec((tm, Q_LORA), lambda i: (i, 0)),
                  pl.BlockSpec((tm, KV_LORA), lambda i: (i, 1)),
                  pl.BlockSpec((tm, LANES), lambda i: (i, kr_block)),
                  pl.BlockSpec((1, Q_LORA), const),
                  pl.BlockSpec((1, KV_LORA), const),
                  pl.BlockSpec(wqt.shape, const),
                  pl.BlockSpec(wk.shape, const),
                  pl.BlockSpec(wvt.shape, const),
                  pl.BlockSpec((tm, LANES), lambda i: (i, 0)),
                  pl.BlockSpec((LANES, tm), lambda i: (0, i))],
        out_specs=[pl.BlockSpec((nq, tm), lambda i: (0, i)),
                   pl.BlockSpec((tm, nq), lambda i: (i, 0)),
                   pl.BlockSpec((MLA_WIDTH, tm), lambda i: (0, i))],
        out_shape=[jax.ShapeDtypeStruct((nq, t), MXU_DTYPE),
                   jax.ShapeDtypeStruct((t, nq), MXU_DTYPE),
                   jax.ShapeDtypeStruct((MLA_WIDTH, t), MXU_DTYPE)],
        compiler_params=_params("arbitrary"),
        name="mla_up",
    )(proj, proj, proj, q_norm.reshape(1, -1), kv_norm.reshape(1, -1), wqt, wk, wvt, tab, tabt)


def _attn_kernel(*refs, n_segments, tk):
    qt_ref = refs[0]
    k_refs = refs[1:1 + n_segments]
    vt_refs = refs[1 + n_segments:1 + 2 * n_segments]
    z_ref, o_ref = refs[1 + 2 * n_segments:]
    qt = qt_ref[...]
    tq = qt.shape[1]
    m = jnp.full((1, tq), -jnp.inf, F32)
    l = jnp.zeros((1, tq), F32)
    acc = jnp.zeros((MLA_V, tq), F32)
    tiles = [(k_ref, vt_ref, j * tk) for k_ref, vt_ref in zip(k_refs, vt_refs)
             for j in range(k_ref.shape[0] // tk)]

    def scores(tile):
        k_ref, _, off = tile
        return jnp.dot(k_ref[off:off + tk, :], qt, preferred_element_type=F32)

    s_next = scores(tiles[0])
    for n, (_, vt_ref, off) in enumerate(tiles):
        s = s_next
        if n + 1 < len(tiles):
            s_next = scores(tiles[n + 1])
        m_new = jnp.maximum(m, jnp.max(s, axis=0, keepdims=True))
        alpha = jnp.exp2(m - m_new)
        p = jnp.exp2(s - m_new)
        l = alpha * l + jnp.sum(p, axis=0, keepdims=True)
        acc = alpha * acc + jnp.dot(vt_ref[:, off:off + tk], p.astype(MXU_DTYPE), preferred_element_type=F32)
        m = m_new
    o = (acc / l).T
    o_ref[...] = (o * _silu(z_ref[...])).astype(o_ref.dtype)


def _attention(qt, k, vt, proj, geom):
    seq, ctx_len = geom.n_lat * ROW_TILE, geom.n_ctx * ROW_TILE
    ctx_block0 = geom.batch * seq // ctx_len
    z_block0 = (Q_LORA + KV_LORA) // LANES
    k_lat = pl.BlockSpec((seq, MLA_HEAD_COLS), lambda b, h, i: (b, h))
    k_ctx = pl.BlockSpec((ctx_len, MLA_HEAD_COLS), lambda b, h, i: (ctx_block0 + b, h))
    vt_lat = pl.BlockSpec((MLA_V, seq), lambda b, h, i: (h, b))
    vt_ctx = pl.BlockSpec((MLA_V, ctx_len), lambda b, h, i: (h, ctx_block0 + b))
    sem = _params("arbitrary", "arbitrary", "arbitrary")

    tq = ATTN_TQ
    nq = seq // tq
    lat = pl.pallas_call(
        functools.partial(_attn_kernel, n_segments=2, tk=ATTN_TK),
        grid=(geom.batch, MLA_HEADS, nq),
        in_specs=[pl.BlockSpec((MLA_HEAD_COLS, tq), lambda b, h, i: (h, b * nq + i)),
                  k_ctx, k_lat, vt_ctx, vt_lat,
                  pl.BlockSpec((tq, MLA_V), lambda b, h, i: (b * nq + i, z_block0 + h))],
        out_specs=pl.BlockSpec((tq, MLA_V), lambda b, h, i: (b * nq + i, h)),
        out_shape=jax.ShapeDtypeStruct((geom.batch * seq, MLA_WIDTH), MXU_DTYPE),
        compiler_params=sem, name="mla_attention",
    )(qt, k, k, vt, vt, proj)
    ctx = pl.pallas_call(
        functools.partial(_attn_kernel, n_segments=1, tk=ATTN_TK),
        grid=(geom.batch, MLA_HEADS, 1),
        in_specs=[pl.BlockSpec((MLA_HEAD_COLS, ctx_len), lambda b, h, i: (h, ctx_block0 + b)),
                  k_ctx, vt_ctx,
                  pl.BlockSpec((ctx_len, MLA_V), lambda b, h, i: (ctx_block0 + b, z_block0 + h))],
        out_specs=pl.BlockSpec((ctx_len, MLA_V), lambda b, h, i: (b, h)),
        out_shape=jax.ShapeDtypeStruct((geom.batch * ctx_len, MLA_WIDTH), MXU_DTYPE),
        compiler_params=sem, name="mla_attention_ctx",
    )(qt, k, vt, proj)
    return jnp.concatenate([lat, ctx], axis=0)


def _gdn_conv_kernel(cur_ref, prev_ref, next_ref, w_ref, o_ref, pad_ref, *, geom, tc):
    c = pl.program_id(1)
    pos, n_tiles = geom.seq_pos(pl.program_id(0))
    pad_ref[0:SUBLANES, :] = jnp.where(pos != 0, prev_ref[...], 0.0)
    pad_ref[SUBLANES:SUBLANES + ROW_TILE, :] = cur_ref[...]
    pad_ref[SUBLANES + ROW_TILE:, :] = jnp.where(pos != n_tiles - 1, next_ref[...], 0.0)
    w = w_ref[...]
    y = jnp.zeros((ROW_TILE, tc), F32)
    for k in range(CONV_W):
        start = SUBLANES - CONV_W // 2 + k
        y = y + pad_ref[start:start + ROW_TILE, :] * w[k:k + 1, :]
    y = _silu(y)
    key_tiles = GDN_KEY // tc

    @pl.when(c < 2 * key_tiles)
    def _():
        scale = jnp.where(c < key_tiles, GDN_DK ** -0.5, 1.0)
        for g in range(tc // GDN_DK):
            yg = y[:, g * GDN_DK:(g + 1) * GDN_DK]
            inv = lax.rsqrt(jnp.sum(yg * yg, axis=-1, keepdims=True) + EPS) * scale
            o_ref[:, g * GDN_DK:(g + 1) * GDN_DK] = (yg * inv).astype(o_ref.dtype)

    @pl.when(c >= 2 * key_tiles)
    def _():
        o_ref[...] = y.astype(o_ref.dtype)


def _gdn_conv(proj, conv_w, geom):
    t = proj.shape[0]
    tc = 512
    halo_per_tile = ROW_TILE // SUBLANES
    last_halo = t // SUBLANES - 1
    return pl.pallas_call(
        functools.partial(_gdn_conv_kernel, geom=geom, tc=tc),
        grid=(t // ROW_TILE, GDN_CONV_CH // tc),
        in_specs=[pl.BlockSpec((ROW_TILE, tc), lambda i, c: (i, c)),
                  pl.BlockSpec((SUBLANES, tc), lambda i, c: (jnp.maximum(i * halo_per_tile - 1, 0), c)),
                  pl.BlockSpec((SUBLANES, tc), lambda i, c: (jnp.minimum((i + 1) * halo_per_tile, last_halo), c)),
                  pl.BlockSpec((CONV_W, tc), lambda i, c: (0, c))],
        out_specs=pl.BlockSpec((ROW_TILE, tc), lambda i, c: (i, c)),
        out_shape=jax.ShapeDtypeStruct((t, GDN_CONV_CH), MXU_DTYPE),
        scratch_shapes=[pltpu.VMEM((ROW_TILE + 2 * SUBLANES, tc), F32)],
        compiler_params=_params("arbitrary", "arbitrary"),
        name="gdn_conv",
    )(proj, proj, proj, conv_w)


def _split3(x):
    hi = x.astype(MXU_DTYPE)
    r = x - hi.astype(F32)
    mid = r.astype(MXU_DTYPE)
    lo = (r - mid.astype(F32)).astype(MXU_DTYPE)
    return hi, mid, lo


def _gdn_gates_kernel(ba_ref, alog_ref, dtb_ref, gb_ref, gbt_ref):
    half = LANES // 2
    ba = ba_ref[...]
    beta = _sigmoid(ba[:, :half])
    x = ba[:, half:] + dtb_ref[...]
    softplus = jnp.maximum(x, 0.0) + jnp.log(1.0 + jnp.exp(-jnp.abs(x)))
    g = -jnp.exp(alog_ref[...]) * softplus
    row = lax.broadcasted_iota(jnp.int32, (ROW_TILE, ROW_TILE), 0)
    col = lax.broadcasted_iota(jnp.int32, (ROW_TILE, ROW_TILE), 1)
    lower = (row >= col).astype(MXU_DTYPE)
    upper = (row <= col).astype(MXU_DTYPE)
    pre = jnp.zeros((ROW_TILE, half), F32)
    suf = jnp.zeros((ROW_TILE, half), F32)
    for piece in _split3(g):
        pre = pre + jnp.dot(lower, piece, preferred_element_type=F32)
        suf = suf + jnp.dot(upper, piece, preferred_element_type=F32)
    lane = lax.broadcasted_iota(jnp.int32, (ROW_TILE, half), 1)
    gcum = jnp.where(lane < half // 2, pre, suf)
    gb = jnp.concatenate([beta, gcum], axis=-1)
    gb_ref[...] = gb
    gbt_ref[...] = gb.T


def _gdn_gates(ba, a_log, dt_bias):
    t = ba.shape[0]
    half = LANES // 2
    return pl.pallas_call(
        _gdn_gates_kernel,
        grid=(t // ROW_TILE,),
        in_specs=[pl.BlockSpec((ROW_TILE, LANES), lambda i: (i, 0)),
                  pl.BlockSpec((1, half), lambda i: (0, 0)),
                  pl.BlockSpec((1, half), lambda i: (0, 0))],
        out_specs=[pl.BlockSpec((ROW_TILE, LANES), lambda i: (i, 0)),
                   pl.BlockSpec((LANES, ROW_TILE), lambda i: (0, i))],
        out_shape=[jax.ShapeDtypeStruct((t, LANES), F32), jax.ShapeDtypeStruct((LANES, t), F32)],
        compiler_params=_params("arbitrary"),
        name="gdn_gates",
    )(ba, a_log.reshape(1, half), dt_bias.reshape(1, half))


def _mxu_dot(a, b):
    return jnp.dot(a.astype(MXU_DTYPE), b.astype(MXU_DTYPE), preferred_element_type=F32)


def _gdn_scan_kernel(q_ref, k_ref, v_ref, gb_ref, gbt_ref, o_ref, state_ref):
    n = ROW_TILE
    hk = pl.program_id(1)
    d = pl.program_id(2)

    @pl.when(pl.program_id(3) == 0)
    def _():
        state_ref[...] = jnp.zeros(state_ref.shape, F32)

    q = q_ref[...]
    k = k_ref[...]
    gram = _nt_dot(jnp.concatenate([q, k], axis=0), k)
    qk, kk = gram[:n], gram[n:]
    row = lax.broadcasted_iota(jnp.int32, (n, n), 0)
    col = lax.broadcasted_iota(jnp.int32, (n, n), 1)
    fwd = d == 0
    ahead = (row - col) * jnp.where(fwd, 1, -1)
    incl = ahead >= 0
    strict = ahead > 0
    eye = (row == col).astype(F32)
    last = jnp.where(fwd, n - 1, 0)
    lane = lax.broadcasted_iota(jnp.int32, (n, LANES), 1)
    sub1 = lax.broadcasted_iota(jnp.int32, (n, 1), 0)
    gb = gb_ref[...]
    half = LANES // 2
    for e in range(GDN_V_HEADS // GDN_K_HEADS):
        c_beta = d * GDN_V_HEADS + hk * (GDN_V_HEADS // GDN_K_HEADS) + e
        c_g = half + c_beta
        beta_c = jnp.sum(jnp.where(lane == c_beta, gb, 0.0), axis=-1, keepdims=True)
        g_c = jnp.sum(jnp.where(lane == c_g, gb, 0.0), axis=-1, keepdims=True)
        beta_r = gbt_ref[pl.ds(c_beta, 1), :]
        g_r = gbt_ref[pl.ds(c_g, 1), :]
        g_last = jnp.sum(jnp.where(sub1 == last, g_c, 0.0), axis=0, keepdims=True)
        decay = jnp.exp(jnp.where(incl, g_c - g_r, -jnp.inf))
        a_mat = jnp.where(strict, beta_c * kk * decay, 0.0)
        block_xor = row ^ col
        inv = eye - jnp.where(block_xor == 1, a_mat, 0.0)
        for log_m in range(1, int(np.log2(n))):
            a_off = jnp.where((block_xor >> log_m) == 1, a_mat, 0.0)
            inv = inv - _mxu_dot(inv, _mxu_dot(a_off, inv))
        eg_r = jnp.exp(g_r)
        u = _mxu_dot(inv * beta_r, v_ref[:, e * GDN_DV:(e + 1) * GDN_DV])
        w = _mxu_dot(inv * (beta_r * eg_r), k)
        state = state_ref[e]
        v_new = u - _mxu_dot(w, state)
        q_dec = q.astype(F32) * jnp.exp(g_c)
        k_dec = k.astype(F32) * jnp.exp(g_last - g_c)
        o = _mxu_dot(q_dec, state) + _mxu_dot(qk * decay, v_new)
        o_ref[0, :, e * GDN_DV:(e + 1) * GDN_DV] = o
        state_ref[e] = state * jnp.exp(g_last) + lax.dot_general(
            k_dec.astype(MXU_DTYPE), v_new.astype(MXU_DTYPE), (((0,), (0,)), ((), ())),
            preferred_element_type=F32)


def _gdn_scan(qkv, gb, gbt, geom):
    t = qkv.shape[0]
    rep = GDN_V_HEADS // GDN_K_HEADS

    def tile(b, d, i):
        ctx_pos = jnp.where(d == 0, i, geom.n_ctx - 1 - i)
        lat_pos = jnp.where(d == 0, i - geom.n_ctx, geom.n_lat - 1 - (i - geom.n_ctx))
        return jnp.where(i < geom.n_ctx, geom.lat_tiles + b * geom.n_ctx + ctx_pos, b * geom.n_lat + lat_pos)

    return pl.pallas_call(
        _gdn_scan_kernel,
        grid=(geom.batch, GDN_K_HEADS, 2, geom.n_ctx + geom.n_lat),
        in_specs=[pl.BlockSpec((ROW_TILE, GDN_DK), lambda b, h, d, i: (tile(b, d, i), h)),
                  pl.BlockSpec((ROW_TILE, GDN_DK), lambda b, h, d, i: (tile(b, d, i), GDN_K_HEADS + h)),
                  pl.BlockSpec((ROW_TILE, rep * GDN_DV), lambda b, h, d, i: (tile(b, d, i), GDN_K_HEADS + h)),
                  pl.BlockSpec((ROW_TILE, LANES), lambda b, h, d, i: (tile(b, d, i), 0)),
                  pl.BlockSpec((LANES, ROW_TILE), lambda b, h, d, i: (0, tile(b, d, i)))],
        out_specs=pl.BlockSpec((1, ROW_TILE, rep * GDN_DV), lambda b, h, d, i: (d, tile(b, d, i), h)),
        out_shape=jax.ShapeDtypeStruct((2, t, GDN_VAL), F32),
        scratch_shapes=[pltpu.VMEM((rep, GDN_DK, GDN_DV), F32)],
        compiler_params=_params("arbitrary", "arbitrary", "arbitrary", "arbitrary"),
        name="gdn_scan",
    )(qkv, qkv, qkv, gb, gbt)


def _gdn_gate_kernel(o_ref, z_ref, nw_ref, a_ref):
    o = o_ref[0] + o_ref[1]
    for g in range(o.shape[1] // GDN_DV):
        cols = slice(g * GDN_DV, (g + 1) * GDN_DV)
        og = o[:, cols]
        y = og * lax.rsqrt(jnp.mean(og * og, axis=-1, keepdims=True) + EPS) * nw_ref[...]
        a_ref[:, cols] = (y * _silu(z_ref[:, cols])).astype(a_ref.dtype)


def _gdn_gate(o, proj, o_norm):
    t = o.shape[1]
    tc = 512
    z_block0 = GDN_CONV_CH // tc
    return pl.pallas_call(
        _gdn_gate_kernel,
        grid=(t // ROW_TILE, GDN_VAL // tc),
        in_specs=[pl.BlockSpec((2, ROW_TILE, tc), lambda i, c: (0, i, c)),
                  pl.BlockSpec((ROW_TILE, tc), lambda i, c: (i, z_block0 + c)),
                  pl.BlockSpec((1, GDN_DV), lambda i, c: (0, 0))],
        out_specs=pl.BlockSpec((ROW_TILE, tc), lambda i, c: (i, c)),
        out_shape=jax.ShapeDtypeStruct((t, GDN_VAL), MXU_DTYPE),
        compiler_params=_params("arbitrary", "arbitrary"),
        name="gdn_gate",
    )(o, proj, o_norm.reshape(1, GDN_DV))


def _final_norm_kernel(x_ref, w_ref, o_ref):
    x = x_ref[...]
    o_ref[...] = x * lax.rsqrt(jnp.mean(x * x, axis=-1, keepdims=True) + EPS) * w_ref[...]


def _final_norm(h, w, geom):
    d = h.shape[1]
    return pl.pallas_call(
        _final_norm_kernel,
        grid=(geom.lat_tiles,),
        in_specs=[pl.BlockSpec((ROW_TILE, d), lambda i: (i, 0)),
                  pl.BlockSpec((1, d), lambda i: (0, 0))],
        out_specs=pl.BlockSpec((ROW_TILE, d), lambda i: (i, 0)),
        out_shape=jax.ShapeDtypeStruct((geom.lat_tiles * ROW_TILE, d), F32),
        compiler_params=_params("arbitrary"),
        name="final_norm",
    )(h, w.reshape(1, d))


def _rope_table(batch, ctx_len, seq):
    pos = jnp.arange(seq)
    r = (pos // GRID_W).astype(F32)
    col = (pos % GRID_W).astype(F32)
    half = MLA_ROPE // 2
    inv = ROPE_THETA ** (-jnp.arange(0, half, 2, dtype=F32) / half)
    ang = jnp.concatenate([r[:, None] * inv, col[:, None] * inv], axis=-1)
    cos, sin = jnp.cos(ang), jnp.sin(ang)
    lat = jnp.concatenate([cos, cos, -sin, sin], axis=-1)
    ident = jnp.concatenate([jnp.ones((ctx_len, MLA_ROPE), F32), jnp.zeros((ctx_len, MLA_ROPE), F32)], axis=-1)
    return jnp.concatenate([jnp.tile(lat, (batch, 1)), jnp.tile(ident, (batch, 1))], axis=0)


_ROPE_SWAP = np.concatenate([np.arange(MLA_ROPE // 2, MLA_ROPE), np.arange(MLA_ROPE // 2)])


def _mla_weights(w_in, w_uq, w_ukv):
    lo = Q_LORA + KV_LORA
    k_rope = w_in[:, lo:lo + MLA_ROPE]
    w_in_p = jnp.concatenate([w_in[:, :lo], w_in[:, lo + MLA_ROPE:], k_rope, k_rope[:, _ROPE_SWAP]], axis=1)
    head = np.concatenate([np.arange(MLA_QK), MLA_NOPE + _ROPE_SWAP])
    cols = (np.arange(MLA_HEADS)[:, None] * MLA_QK + head[None, :]).reshape(-1)
    wqt = w_uq[:, cols].T.astype(MXU_DTYPE)
    w_kv = w_ukv.reshape(KV_LORA, MLA_HEADS, MLA_NOPE + MLA_V)
    wk = w_kv[:, :, :MLA_NOPE].reshape(KV_LORA, MLA_HEADS * MLA_NOPE).astype(MXU_DTYPE)
    wvt = w_kv[:, :, MLA_NOPE:].reshape(KV_LORA, MLA_WIDTH).T.astype(MXU_DTYPE)
    return w_in_p, wqt, wk, wvt


def _mla_layer(h, a, mod, tab, tabt, w_in, q_norm, w_uq, kv_norm, w_ukv, w_o, geom):
    w_in_p, wqt, wk, wvt = _mla_weights(w_in, w_uq, w_ukv)
    proj = _matmul(a, w_in_p, tn=640, name="mla_in")
    qt, k, vt = _mla_up(proj, q_norm, kv_norm, wqt, wk, wvt, tab, tabt)
    gated = _attention(qt, k, vt, proj, geom)
    return _matmul_residual(gated, w_o, h, mod, geom, name="mla_out")


def _gdn_layer(h, a, mod, w_in, conv_w, a_log, dt_bias, o_norm, w_o, geom):
    proj = _matmul(a, w_in, tn=1024, n_blocks=GDN_QKVZ // 1024, name="gdn_in")
    ba = _matmul(a, w_in, tn=LANES, first_block=GDN_QKVZ // LANES, n_blocks=1, name="gdn_in_gates")
    qkv = _gdn_conv(proj, conv_w, geom)
    gb, gbt = _gdn_gates(ba, a_log, dt_bias)
    o = _gdn_scan(qkv, gb, gbt, geom)
    gated = _gdn_gate(o, proj, o_norm)
    return _matmul_residual(gated, w_o, h, mod, geom, name="gdn_out")


def kernel(x, c, ctx, c_ctx, ada_w, ada_b, norm_w, mla_w_in, mla_q_norm, mla_w_uq, mla_kv_norm, mla_w_ukv, mla_w_o, gdn_w_in, gdn_conv_w, gdn_a_log, gdn_dt_bias, gdn_o_norm, gdn_w_o, final_norm):
    batch, seq, d = x.shape
    ctx_len = ctx.shape[1]
    n_rows = batch * (seq + ctx_len)
    assert d == D_MODEL and seq % ATTN_TQ == 0 and ctx_len % ROW_TILE == 0 and seq % GRID_W == 0
    assert batch < SUBLANES and n_rows % MM_TM == 0 and (batch * seq) % ctx_len == 0
    geom = Geom(batch=batch, n_lat=seq // ROW_TILE, n_ctx=ctx_len // ROW_TILE)

    h = jnp.concatenate([x.reshape(batch * seq, d), ctx.reshape(batch * ctx_len, d)], axis=0)
    conds = jnp.zeros((SUBLANES, d), F32).at[:batch].set(c).at[batch].set(c_ctx)
    mods = _adaln(conds, ada_w, ada_b)
    tab = _rope_table(batch, ctx_len, seq)
    tabt = tab.T

    for i in range(DEPTH):
        a = _prenorm(h, norm_w[i], mods[i], geom)
        j = i // 2
        if i % 2 == 0:
            h = _mla_layer(h, a, mods[i], tab, tabt, mla_w_in[j], mla_q_norm[j], mla_w_uq[j], mla_kv_norm[j],
                           mla_w_ukv[j], mla_w_o[j], geom)
        else:
            h = _gdn_layer(h, a, mods[i], gdn_w_in[j], gdn_conv_w[j], gdn_a_log[j], gdn_dt_bias[j],
                           gdn_o_norm[j], gdn_w_o[j], geom)
    return _final_norm(h, final_norm, geom).reshape(batch, seq, d)
```

```python
import functools
import math
from typing import NamedTuple

import jax
import jax.numpy as jnp
import numpy as np
from jax import lax
from jax.experimental import pallas as pl
from jax.experimental.pallas import tpu as pltpu

F32 = jnp.float32
MXU_DTYPE = jnp.bfloat16

D_MODEL = 2048
DEPTH = 4
EPS = 1e-6
GRID_W = 64
ROPE_THETA = 10000.0
MLA_HEADS = 16
MLA_NOPE = 128
MLA_ROPE = 64
MLA_V = 128
MLA_QK = MLA_NOPE + MLA_ROPE
Q_LORA = 512
KV_LORA = 512
MLA_WIDTH = MLA_HEADS * MLA_V
MLA_HEAD_COLS = 256
MLA_IN_PAD = Q_LORA + KV_LORA + MLA_WIDTH + 2 * MLA_ROPE
GDN_K_HEADS = 16
GDN_V_HEADS = 32
GDN_DK = 128
GDN_DV = 128
GDN_KEY = GDN_K_HEADS * GDN_DK
GDN_VAL = GDN_V_HEADS * GDN_DV
GDN_CONV_CH = 2 * GDN_KEY + GDN_VAL
GDN_QKVZ = GDN_CONV_CH + GDN_VAL
CONV_W = 5

LANES = 128
SUBLANES = 8
ROW_TILE = 256
MM_TM = 512
ATTN_TQ = 512
ATTN_TK = 256
GDN_SCAN_K_HEADS = 4
VMEM_LIMIT_BYTES = 50 * 1024 * 1024


class Geom(NamedTuple):
    batch: int
    n_lat: int
    n_ctx: int

    @property
    def lat_tiles(self):
        return self.batch * self.n_lat

    def mod_row(self, tile):
        return jnp.where(tile < self.lat_tiles, tile // self.n_lat, self.batch)

    def seq_pos(self, tile):
        is_lat = tile < self.lat_tiles
        pos = jnp.where(is_lat, tile % self.n_lat, (tile - self.lat_tiles) % self.n_ctx)
        return pos, jnp.where(is_lat, self.n_lat, self.n_ctx)


def _params(*sem):
    return pltpu.CompilerParams(dimension_semantics=sem, vmem_limit_bytes=VMEM_LIMIT_BYTES)


def _sigmoid(x):
    return 1.0 / (1.0 + jnp.exp(-x))


def _silu(x):
    return x * _sigmoid(x)


def _nt_dot(a, b):
    return lax.dot_general(a, b, (((1,), (1,)), ((), ())), preferred_element_type=F32)


def _adaln_kernel(c_ref, w_ref, b_ref, o_ref):
    a = _silu(c_ref[...]).astype(MXU_DTYPE)
    w = w_ref[0].astype(MXU_DTYPE)
    o_ref[0] = jnp.dot(a, w, preferred_element_type=F32) + b_ref[0]


def _adaln(conds, ada_w, ada_b):
    depth, d, n = ada_w.shape
    tn = 768
    return pl.pallas_call(
        _adaln_kernel,
        grid=(depth, n // tn),
        in_specs=[pl.BlockSpec((SUBLANES, d), lambda l, j: (0, 0)),
                  pl.BlockSpec((1, d, tn), lambda l, j: (l, 0, j)),
                  pl.BlockSpec((1, 1, tn), lambda l, j: (l, 0, j))],
        out_specs=pl.BlockSpec((1, SUBLANES, tn), lambda l, j: (l, 0, j)),
        out_shape=jax.ShapeDtypeStruct((depth, SUBLANES, n), F32),
        compiler_params=_params("arbitrary", "arbitrary"),
        name="adaln",
    )(conds, ada_w, ada_b.reshape(depth, 1, n))


def _prenorm_kernel(x_ref, nw_ref, mod_ref, o_ref, *, geom):
    m = mod_ref[pl.ds(geom.mod_row(pl.program_id(0)), 1), :]
    x = x_ref[...]
    y = x * lax.rsqrt(jnp.mean(x * x, axis=-1, keepdims=True) + EPS) * nw_ref[...]
    o_ref[...] = (y * (1.0 + m[:, D_MODEL:2 * D_MODEL]) + m[:, :D_MODEL]).astype(o_ref.dtype)


def _prenorm(h, norm_w, mod, geom):
    t, d = h.shape
    return pl.pallas_call(
        functools.partial(_prenorm_kernel, geom=geom),
        grid=(t // ROW_TILE,),
        in_specs=[pl.BlockSpec((ROW_TILE, d), lambda i: (i, 0)),
                  pl.BlockSpec((1, d), lambda i: (0, 0)),
                  pl.BlockSpec((SUBLANES, 3 * d), lambda i: (0, 0))],
        out_specs=pl.BlockSpec((ROW_TILE, d), lambda i: (i, 0)),
        out_shape=jax.ShapeDtypeStruct((t, d), MXU_DTYPE),
        compiler_params=_params("arbitrary"),
        name="prenorm",
    )(h, norm_w.reshape(1, d), mod)


def _mm_kernel(a_ref, w_ref, o_ref, w_mxu_ref):
    @pl.when(pl.program_id(1) == 0)
    def _():
        w_mxu_ref[...] = w_ref[...].astype(MXU_DTYPE)

    o_ref[...] = jnp.dot(a_ref[...], w_mxu_ref[...], preferred_element_type=F32).astype(o_ref.dtype)


def _matmul(a, w, *, tn, first_block=0, n_blocks=None, out_dtype=F32, name="matmul"):
    m, k = a.shape
    n_blocks = w.shape[1] // tn if n_blocks is None else n_blocks
    return pl.pallas_call(
        _mm_kernel,
        grid=(n_blocks, m // MM_TM),
        in_specs=[pl.BlockSpec((MM_TM, k), lambda j, i: (i, 0)),
                  pl.BlockSpec((k, tn), lambda j, i: (0, first_block + j))],
        out_specs=pl.BlockSpec((MM_TM, tn), lambda j, i: (i, j)),
        out_shape=jax.ShapeDtypeStruct((m, n_blocks * tn), out_dtype),
        scratch_shapes=[pltpu.VMEM((k, tn), MXU_DTYPE)],
        compiler_params=_params("arbitrary", "arbitrary"),
        name=name,
    )(a, w)


def _mm_res_kernel(a_ref, w_ref, res_ref, gate_ref, o_ref, w_mxu_ref, *, geom):
    @pl.when(pl.program_id(1) == 0)
    def _():
        w_mxu_ref[...] = w_ref[...].astype(MXU_DTYPE)

    acc = jnp.dot(a_ref[...], w_mxu_ref[...], preferred_element_type=F32)
    for s in range(MM_TM // ROW_TILE):
        tile = pl.program_id(1) * (MM_TM // ROW_TILE) + s
        g = gate_ref[pl.ds(geom.mod_row(tile), 1), :]
        rows = slice(s * ROW_TILE, (s + 1) * ROW_TILE)
        o_ref[rows, :] = res_ref[rows, :] + g * acc[rows, :]


def _matmul_residual(a, w, res, mod, geom, *, tn=512, name="matmul_res"):
    m, k = a.shape
    n = w.shape[1]
    gate_block0 = 2 * D_MODEL // tn
    return pl.pallas_call(
        functools.partial(_mm_res_kernel, geom=geom),
        grid=(n // tn, m // MM_TM),
        in_specs=[pl.BlockSpec((MM_TM, k), lambda j, i: (i, 0)),
                  pl.BlockSpec((k, tn), lambda j, i: (0, j)),
                  pl.BlockSpec((MM_TM, tn), lambda j, i: (i, j)),
                  pl.BlockSpec((SUBLANES, tn), lambda j, i: (0, gate_block0 + j))],
        out_specs=pl.BlockSpec((MM_TM, tn), lambda j, i: (i, j)),
        out_shape=jax.ShapeDtypeStruct((m, n), F32),
        scratch_shapes=[pltpu.VMEM((k, tn), MXU_DTYPE)],
        compiler_params=_params("arbitrary", "arbitrary"),
        name=name,
    )(a, w, res, mod)


def _mla_up_kernel(cq_ref, ckv_ref, kr_ref, qn_ref, kvn_ref, wqt_ref, wk_ref, wvt_ref, tab_ref, tabt_ref,
                   qt_ref, k_ref, vt_ref):
    def rms(x, w):
        return (x * lax.rsqrt(jnp.mean(x * x, axis=-1, keepdims=True) + EPS) * w).astype(MXU_DTYPE)

    cq = rms(cq_ref[...], qn_ref[...])
    ckv = rms(ckv_ref[...], kvn_ref[...])
    tm = cq.shape[0]
    t = kr_ref[...] * tab_ref[...]
    lane = lax.broadcasted_iota(jnp.int32, t.shape, 1)
    k_rot = jnp.where(lane < MLA_ROPE, t + pltpu.roll(t, MLA_ROPE, axis=1), 0.0).astype(k_ref.dtype)
    tabt = tabt_ref[...]
    zeros = jnp.zeros((MLA_ROPE, tm), qt_ref.dtype)
    q_scale = MLA_QK ** -0.5 * math.log2(math.e)
    for h in range(MLA_HEADS):
        q0 = h * MLA_HEAD_COLS
        qh = _nt_dot(wqt_ref[q0:q0 + MLA_HEAD_COLS, :], cq) * q_scale
        qt_ref[q0:q0 + MLA_NOPE, :] = qh[:MLA_NOPE].astype(qt_ref.dtype)
        tq = qh[MLA_NOPE:] * tabt
        qt_ref[q0 + MLA_NOPE:q0 + MLA_NOPE + MLA_ROPE, :] = (tq[:MLA_ROPE] + tq[MLA_ROPE:]).astype(qt_ref.dtype)
        qt_ref[q0 + MLA_NOPE + MLA_ROPE:q0 + MLA_HEAD_COLS, :] = zeros
        k_ref[:, q0:q0 + MLA_NOPE] = jnp.dot(ckv, wk_ref[:, h * MLA_NOPE:(h + 1) * MLA_NOPE],
                                            preferred_element_type=F32).astype(k_ref.dtype)
        k_ref[:, q0 + MLA_NOPE:q0 + MLA_HEAD_COLS] = k_rot
        vt_ref[h * MLA_V:(h + 1) * MLA_V, :] = _nt_dot(wvt_ref[h * MLA_V:(h + 1) * MLA_V, :], ckv).astype(vt_ref.dtype)


def _mla_up(proj, q_norm, kv_norm, wqt, wk, wvt, tab, tabt):
    t = proj.shape[0]
    tm = ROW_TILE
    nq = MLA_HEADS * MLA_HEAD_COLS
    kr_block = (Q_LORA + KV_LORA + MLA_WIDTH) // LANES
    const = lambda i: (0, 0)
    return pl.pallas_call(
        _mla_up_kernel,
        grid=(t // tm,),
        in_specs=[pl.BlockSpec((tm, Q_LORA), lambda i: (i, 0)),
                  pl.BlockSpec((tm, KV_LORA), lambda i: (i, 1)),
                  pl.BlockSpec((tm, LANES), lambda i: (i, kr_block)),
                  pl.BlockSpec((1, Q_LORA), const),
                  pl.BlockSpec((1, KV_LORA), const),
                  pl.BlockSpec(wqt.shape, const),
                  pl.BlockSpec(wk.shape, const),
                  pl.BlockSpec(wvt.shape, const),
                  pl.BlockSpec((tm, LANES), lambda i: (i, 0)),
                  pl.BlockSpec((LANES, tm), lambda i: (0, i))],
        out_specs=[pl.BlockSpec((nq, tm), lambda i: (0, i)),
                   pl.BlockSpec((tm, nq), lambda i: (i, 0)),
                   pl.BlockSpec((MLA_WIDTH, tm), lambda i: (0, i))],
        out_shape=[jax.ShapeDtypeStruct((nq, t), MXU_DTYPE),
                   jax.ShapeDtypeStruct((t, nq), MXU_DTYPE),
                   jax.ShapeDtypeStruct((MLA_WIDTH, t), MXU_DTYPE)],
        compiler_params=_params("arbitrary"),
        name="mla_up",
    )(proj, proj, proj, q_norm.reshape(1, -1), kv_norm.reshape(1, -1), wqt, wk, wvt, tab, tabt)


def _attn_kernel(*refs, n_segments, tk):
    qt_ref = refs[0]
    k_refs = refs[1:1 + n_segments]
    vt_refs = refs[1 + n_segments:1 + 2 * n_segments]
    z_ref, o_ref = refs[1 + 2 * n_segments:]
    qt = qt_ref[...]
    tq = qt.shape[1]
    m = jnp.full((1, tq), -jnp.inf, F32)
    l = jnp.zeros((1, tq), F32)
    acc = jnp.zeros((MLA_V, tq), F32)
    tiles = [(k_ref, vt_ref, j * tk) for k_ref, vt_ref in zip(k_refs, vt_refs)
             for j in range(k_ref.shape[0] // tk)]

    def scores(tile):
        k_ref, _, off = tile
        return jnp.dot(k_ref[off:off + tk, :], qt, preferred_element_type=F32)

    s_next = scores(tiles[0])
    for n, (_, vt_ref, off) in enumerate(tiles):
        s = s_next
        if n + 1 < len(tiles):
            s_next = scores(tiles[n + 1])
        m_new = jnp.maximum(m, jnp.max(s, axis=0, keepdims=True))
        alpha = jnp.exp2(m - m_new)
        p = jnp.exp2(s - m_new)
        l = alpha * l + jnp.sum(p, axis=0, keepdims=True)
        acc = alpha * acc + jnp.dot(vt_ref[:, off:off + tk], p.astype(MXU_DTYPE), preferred_element_type=F32)
        m = m_new
    o = (acc / l).T
    o_ref[...] = (o * _silu(z_ref[...])).astype(o_ref.dtype)


def _attention(qt, k, vt, proj, geom):
    seq, ctx_len = geom.n_lat * ROW_TILE, geom.n_ctx * ROW_TILE
    ctx_block0 = geom.batch * seq // ctx_len
    z_block0 = (Q_LORA + KV_LORA) // LANES
    k_lat = pl.BlockSpec((seq, MLA_HEAD_COLS), lambda b, h, i: (b, h))
    k_ctx = pl.BlockSpec((ctx_len, MLA_HEAD_COLS), lambda b, h, i: (ctx_block0 + b, h))
    vt_lat = pl.BlockSpec((MLA_V, seq), lambda b, h, i: (h, b))
    vt_ctx = pl.BlockSpec((MLA_V, ctx_len), lambda b, h, i: (h, ctx_block0 + b))
    sem = _params("arbitrary", "arbitrary", "arbitrary")

    tq = ATTN_TQ
    nq = seq // tq
    lat = pl.pallas_call(
        functools.partial(_attn_kernel, n_segments=2, tk=ATTN_TK),
        grid=(geom.batch, MLA_HEADS, nq),
        in_specs=[pl.BlockSpec((MLA_HEAD_COLS, tq), lambda b, h, i: (h, b * nq + i)),
                  k_ctx, k_lat, vt_ctx, vt_lat,
                  pl.BlockSpec((tq, MLA_V), lambda b, h, i: (b * nq + i, z_block0 + h))],
        out_specs=pl.BlockSpec((tq, MLA_V), lambda b, h, i: (b * nq + i, h)),
        out_shape=jax.ShapeDtypeStruct((geom.batch * seq, MLA_WIDTH), MXU_DTYPE),
        compiler_params=sem, name="mla_attention",
    )(qt, k, k, vt, vt, proj)
    ctx = pl.pallas_call(
        functools.partial(_attn_kernel, n_segments=1, tk=ATTN_TK),
        grid=(geom.batch, MLA_HEADS, 1),
        in_specs=[pl.BlockSpec((MLA_HEAD_COLS, ctx_len), lambda b, h, i: (h, ctx_block0 + b)),
                  k_ctx, vt_ctx,
                  pl.BlockSpec((ctx_len, MLA_V), lambda b, h, i: (ctx_block0 + b, z_block0 + h))],
        out_specs=pl.BlockSpec((ctx_len, MLA_V), lambda b, h, i: (b, h)),
        out_shape=jax.ShapeDtypeStruct((geom.batch * ctx_len, MLA_WIDTH), MXU_DTYPE),
        compiler_params=sem, name="mla_attention_ctx",
    )(qt, k, vt, proj)
    return jnp.concatenate([lat, ctx], axis=0)


def _gdn_conv_kernel(cur_ref, prev_ref, next_ref, w_ref, o_ref, pad_ref, *, geom, tc):
    c = pl.program_id(1)
    pos, n_tiles = geom.seq_pos(pl.program_id(0))
    pad_ref[0:SUBLANES, :] = jnp.where(pos != 0, prev_ref[...], 0.0)
    pad_ref[SUBLANES:SUBLANES + ROW_TILE, :] = cur_ref[...]
    pad_ref[SUBLANES + ROW_TILE:, :] = jnp.where(pos != n_tiles - 1, next_ref[...], 0.0)
    w = w_ref[...]
    y = jnp.zeros((ROW_TILE, tc), F32)
    for k in range(CONV_W):
        start = SUBLANES - CONV_W // 2 + k
        y = y + pad_ref[start:start + ROW_TILE, :] * w[k:k + 1, :]
    y = _silu(y)
    key_tiles = GDN_KEY // tc

    @pl.when(c < 2 * key_tiles)
    def _():
        scale = jnp.where(c < key_tiles, GDN_DK ** -0.5, 1.0)
        for g in range(tc // GDN_DK):
            yg = y[:, g * GDN_DK:(g + 1) * GDN_DK]
            inv = lax.rsqrt(jnp.sum(yg * yg, axis=-1, keepdims=True) + EPS) * scale
            o_ref[:, g * GDN_DK:(g + 1) * GDN_DK] = (yg * inv).astype(o_ref.dtype)

    @pl.when(c >= 2 * key_tiles)
    def _():
        o_ref[...] = y.astype(o_ref.dtype)


def _gdn_conv(proj, conv_w, geom):
    t = proj.shape[0]
    tc = 512
    halo_per_tile = ROW_TILE // SUBLANES
    last_halo = t // SUBLANES - 1
    return pl.pallas_call(
        functools.partial(_gdn_conv_kernel, geom=geom, tc=tc),
        grid=(t // ROW_TILE, GDN_CONV_CH // tc),
        in_specs=[pl.BlockSpec((ROW_TILE, tc), lambda i, c: (i, c)),
                  pl.BlockSpec((SUBLANES, tc), lambda i, c: (jnp.maximum(i * halo_per_tile - 1, 0), c)),
                  pl.BlockSpec((SUBLANES, tc), lambda i, c: (jnp.minimum((i + 1) * halo_per_tile, last_halo), c)),
                  pl.BlockSpec((CONV_W, tc), lambda i, c: (0, c))],
        out_specs=pl.BlockSpec((ROW_TILE, tc), lambda i, c: (i, c)),
        out_shape=jax.ShapeDtypeStruct((t, GDN_CONV_CH), MXU_DTYPE),
        scratch_shapes=[pltpu.VMEM((ROW_TILE + 2 * SUBLANES, tc), F32)],
        compiler_params=_params("arbitrary", "arbitrary"),
        name="gdn_conv",
    )(proj, proj, proj, conv_w)


def _split3(x):
    hi = x.astype(MXU_DTYPE)
    r = x - hi.astype(F32)
    mid = r.astype(MXU_DTYPE)
    lo = (r - mid.astype(F32)).astype(MXU_DTYPE)
    return hi, mid, lo


def _gdn_gates_kernel(ba_ref, alog_ref, dtb_ref, gb_ref, gbt_ref):
    half = LANES // 2
    ba = ba_ref[...]
    beta = _sigmoid(ba[:, :half])
    x = ba[:, half:] + dtb_ref[...]
    softplus = jnp.maximum(x, 0.0) + jnp.log(1.0 + jnp.exp(-jnp.abs(x)))
    g = -jnp.exp(alog_ref[...]) * softplus
    row = lax.broadcasted_iota(jnp.int32, (ROW_TILE, ROW_TILE), 0)
    col = lax.broadcasted_iota(jnp.int32, (ROW_TILE, ROW_TILE), 1)
    lower = (row >= col).astype(MXU_DTYPE)
    upper = (row <= col).astype(MXU_DTYPE)
    pre = jnp.zeros((ROW_TILE, half), F32)
    suf = jnp.zeros((ROW_TILE, half), F32)
    for piece in _split3(g):
        pre = pre + jnp.dot(lower, piece, preferred_element_type=F32)
        suf = suf + jnp.dot(upper, piece, preferred_element_type=F32)
    lane = lax.broadcasted_iota(jnp.int32, (ROW_TILE, half), 1)
    gcum = jnp.where(lane < half // 2, pre, suf)
    gb = jnp.concatenate([beta, gcum], axis=-1)
    gb_ref[...] = gb
    gbt_ref[...] = gb.T


def _gdn_gates(ba, a_log, dt_bias):
    t = ba.shape[0]
    half = LANES // 2
    return pl.pallas_call(
        _gdn_gates_kernel,
        grid=(t // ROW_TILE,),
        in_specs=[pl.BlockSpec((ROW_TILE, LANES), lambda i: (i, 0)),
                  pl.BlockSpec((1, half), lambda i: (0, 0)),
                  pl.BlockSpec((1, half), lambda i: (0, 0))],
        out_specs=[pl.BlockSpec((ROW_TILE, LANES), lambda i: (i, 0)),
                   pl.BlockSpec((LANES, ROW_TILE), lambda i: (0, i))],
        out_shape=[jax.ShapeDtypeStruct((t, LANES), F32), jax.ShapeDtypeStruct((LANES, t), F32)],
        compiler_params=_params("arbitrary"),
        name="gdn_gates",
    )(ba, a_log.reshape(1, half), dt_bias.reshape(1, half))


def _mxu_dot(a, b):
    return jnp.dot(a.astype(MXU_DTYPE), b.astype(MXU_DTYPE), preferred_element_type=F32)


def _gdn_scan_kernel(q_ref, k_ref, v_ref, gb_ref, gbt_ref, o_ref, state_ref):
    n = ROW_TILE
    rep = GDN_V_HEADS // GDN_K_HEADS
    d = pl.program_id(2)

    @pl.when(pl.program_id(3) == 0)
    def _():
        state_ref[...] = jnp.zeros(state_ref.shape, F32)

    row = lax.broadcasted_iota(jnp.int32, (n, n), 0)
    col = lax.broadcasted_iota(jnp.int32, (n, n), 1)
    fwd = d == 0
    ahead = (row - col) * jnp.where(fwd, 1, -1)
    incl = ahead >= 0
    strict = ahead > 0
    eye = (row == col).astype(F32)
    block_xor = row ^ col
    last = jnp.where(fwd, n - 1, 0)
    lane = lax.broadcasted_iota(jnp.int32, (n, LANES), 1)
    sub1 = lax.broadcasted_iota(jnp.int32, (n, 1), 0)
    gb = gb_ref[...]
    half = LANES // 2

    probs = []
    for kh in range(GDN_SCAN_K_HEADS):
        q = q_ref[:, kh * GDN_DK:(kh + 1) * GDN_DK]
        k = k_ref[:, kh * GDN_DK:(kh + 1) * GDN_DK]
        gram = _nt_dot(jnp.concatenate([q, k], axis=0), k)
        qk, kk = gram[:n], gram[n:]
        for e in range(rep):
            idx = kh * rep + e
            c_beta = d * GDN_V_HEADS + (pl.program_id(1) * GDN_SCAN_K_HEADS + kh) * rep + e
            c_g = half + c_beta
            beta_c = jnp.sum(jnp.where(lane == c_beta, gb, 0.0), axis=-1, keepdims=True)
            g_c = jnp.sum(jnp.where(lane == c_g, gb, 0.0), axis=-1, keepdims=True)
            beta_r = gbt_ref[pl.ds(c_beta, 1), :]
            g_r = gbt_ref[pl.ds(c_g, 1), :]
            g_last = jnp.sum(jnp.where(sub1 == last, g_c, 0.0), axis=0, keepdims=True)
            decay = jnp.exp(jnp.where(incl, g_c - g_r, -jnp.inf))
            probs.append(dict(
                idx=idx, q=q, k=k, v=v_ref[:, idx * GDN_DV:(idx + 1) * GDN_DV], g_c=g_c, g_last=g_last,
                beta_r=beta_r, eg_r=jnp.exp(g_r), qk_dec=qk * decay,
                a_mat=jnp.where(strict, beta_c * kk * decay, 0.0)))

    invs = [eye - jnp.where(block_xor == 1, p["a_mat"], 0.0) for p in probs]
    for log_m in range(1, int(np.log2(n))):
        off = (block_xor >> log_m) == 1
        xs = [_mxu_dot(jnp.where(off, p["a_mat"], 0.0), inv) for p, inv in zip(probs, invs)]
        invs = [inv - _mxu_dot(inv, x) for inv, x in zip(invs, xs)]

    us = [_mxu_dot(inv * p["beta_r"], p["v"]) for p, inv in zip(probs, invs)]
    ws = [_mxu_dot(inv * (p["beta_r"] * p["eg_r"]), p["k"]) for p, inv in zip(probs, invs)]
    states = [state_ref[p["idx"]] for p in probs]
    q_decs = [p["q"].astype(F32) * jnp.exp(p["g_c"]) for p in probs]
    from_state = [_mxu_dot(jnp.concatenate([w, q_dec], axis=0), s) for w, q_dec, s in zip(ws, q_decs, states)]
    v_news = [u - r[:n] for u, r in zip(us, from_state)]
    outs = [r[n:] + _mxu_dot(p["qk_dec"], v_new) for p, r, v_new in zip(probs, from_state, v_news)]
    for p, s, v_new, o in zip(probs, states, v_news, outs):
        i = p["idx"]
        o_ref[0, :, i * GDN_DV:(i + 1) * GDN_DV] = o
        k_dec = p["k"].astype(F32) * jnp.exp(p["g_last"] - p["g_c"])
        state_ref[i] = s * jnp.exp(p["g_last"]) + lax.dot_general(
            k_dec.astype(MXU_DTYPE), v_new.astype(MXU_DTYPE), (((0,), (0,)), ((), ())),
            preferred_element_type=F32)


def _gdn_scan(qkv, gb, gbt, geom):
    t = qkv.shape[0]
    rep = GDN_V_HEADS // GDN_K_HEADS
    kw = GDN_SCAN_K_HEADS * GDN_DK
    vw = GDN_SCAN_K_HEADS * rep * GDN_DV

    def tile(b, d, i):
        ctx_pos = jnp.where(d == 0, i, geom.n_ctx - 1 - i)
        lat_pos = jnp.where(d == 0, i - geom.n_ctx, geom.n_lat - 1 - (i - geom.n_ctx))
        return jnp.where(i < geom.n_ctx, geom.lat_tiles + b * geom.n_ctx + ctx_pos, b * geom.n_lat + lat_pos)

    return pl.pallas_call(
        _gdn_scan_kernel,
        grid=(geom.batch, GDN_K_HEADS // GDN_SCAN_K_HEADS, 2, geom.n_ctx + geom.n_lat),
        in_specs=[pl.BlockSpec((ROW_TILE, kw), lambda b, h, d, i: (tile(b, d, i), h)),
                  pl.BlockSpec((ROW_TILE, kw), lambda b, h, d, i: (tile(b, d, i), GDN_KEY // kw + h)),
                  pl.BlockSpec((ROW_TILE, vw), lambda b, h, d, i: (tile(b, d, i), 2 * GDN_KEY // vw + h)),
                  pl.BlockSpec((ROW_TILE, LANES), lambda b, h, d, i: (tile(b, d, i), 0)),
                  pl.BlockSpec((LANES, ROW_TILE), lambda b, h, d, i: (0, tile(b, d, i)))],
        out_specs=pl.BlockSpec((1, ROW_TILE, vw), lambda b, h, d, i: (d, tile(b, d, i), h)),
        out_shape=jax.ShapeDtypeStruct((2, t, GDN_VAL), F32),
        scratch_shapes=[pltpu.VMEM((GDN_SCAN_K_HEADS * rep, GDN_DK, GDN_DV), F32)],
        compiler_params=_params("arbitrary", "arbitrary", "arbitrary", "arbitrary"),
        name="gdn_scan",
    )(qkv, qkv, qkv, gb, gbt)


def _gdn_gate_kernel(o_ref, z_ref, nw_ref, a_ref):
    o = o_ref[0] + o_ref[1]
    for g in range(o.shape[1] // GDN_DV):
        cols = slice(g * GDN_DV, (g + 1) * GDN_DV)
        og = o[:, cols]
        y = og * lax.rsqrt(jnp.mean(og * og, axis=-1, keepdims=True) + EPS) * nw_ref[...]
        a_ref[:, cols] = (y * _silu(z_ref[:, cols])).astype(a_ref.dtype)


def _gdn_gate(o, proj, o_norm):
    t = o.shape[1]
    tc = 512
    z_block0 = GDN_CONV_CH // tc
    return pl.pallas_call(
        _gdn_gate_kernel,
        grid=(t // ROW_TILE, GDN_VAL // tc),
        in_specs=[pl.BlockSpec((2, ROW_TILE, tc), lambda i, c: (0, i, c)),
                  pl.BlockSpec((ROW_TILE, tc), lambda i, c: (i, z_block0 + c)),
                  pl.BlockSpec((1, GDN_DV), lambda i, c: (0, 0))],
        out_specs=pl.BlockSpec((ROW_TILE, tc), lambda i, c: (i, c)),
        out_shape=jax.ShapeDtypeStruct((t, GDN_VAL), MXU_DTYPE),
        compiler_params=_params("arbitrary", "arbitrary"),
        name="gdn_gate",
    )(o, proj, o_norm.reshape(1, GDN_DV))


def _final_norm_kernel(x_ref, w_ref, o_ref):
    x = x_ref[...]
    o_ref[...] = x * lax.rsqrt(jnp.mean(x * x, axis=-1, keepdims=True) + EPS) * w_ref[...]


def _final_norm(h, w, geom):
    d = h.shape[1]
    return pl.pallas_call(
        _final_norm_kernel,
        grid=(geom.lat_tiles,),
        in_specs=[pl.BlockSpec((ROW_TILE, d), lambda i: (i, 0)),
                  pl.BlockSpec((1, d), lambda i: (0, 0))],
        out_specs=pl.BlockSpec((ROW_TILE, d), lambda i: (i, 0)),
        out_shape=jax.ShapeDtypeStruct((geom.lat_tiles * ROW_TILE, d), F32),
        compiler_params=_params("arbitrary"),
        name="final_norm",
    )(h, w.reshape(1, d))


def _rope_table(batch, ctx_len, seq):
    pos = jnp.arange(seq)
    r = (pos // GRID_W).astype(F32)
    col = (pos % GRID_W).astype(F32)
    half = MLA_ROPE // 2
    inv = ROPE_THETA ** (-jnp.arange(0, half, 2, dtype=F32) / half)
    ang = jnp.concatenate([r[:, None] * inv, col[:, None] * inv], axis=-1)
    cos, sin = jnp.cos(ang), jnp.sin(ang)
    lat = jnp.concatenate([cos, cos, -sin, sin], axis=-1)
    ident = jnp.concatenate([jnp.ones((ctx_len, MLA_ROPE), F32), jnp.zeros((ctx_len, MLA_ROPE), F32)], axis=-1)
    return jnp.concatenate([jnp.tile(lat, (batch, 1)), jnp.tile(ident, (batch, 1))], axis=0)


_ROPE_SWAP = np.concatenate([np.arange(MLA_ROPE // 2, MLA_ROPE), np.arange(MLA_ROPE // 2)])


def _mla_weights(w_in, w_uq, w_ukv):
    lo = Q_LORA + KV_LORA
    k_rope = w_in[:, lo:lo + MLA_ROPE]
    w_in_p = jnp.concatenate([w_in[:, :lo], w_in[:, lo + MLA_ROPE:], k_rope, k_rope[:, _ROPE_SWAP]], axis=1)
    head = np.concatenate([np.arange(MLA_QK), MLA_NOPE + _ROPE_SWAP])
    cols = (np.arange(MLA_HEADS)[:, None] * MLA_QK + head[None, :]).reshape(-1)
    wqt = w_uq[:, cols].T.astype(MXU_DTYPE)
    w_kv = w_ukv.reshape(KV_LORA, MLA_HEADS, MLA_NOPE + MLA_V)
    wk = w_kv[:, :, :MLA_NOPE].reshape(KV_LORA, MLA_HEADS * MLA_NOPE).astype(MXU_DTYPE)
    wvt = w_kv[:, :, MLA_NOPE:].reshape(KV_LORA, MLA_WIDTH).T.astype(MXU_DTYPE)
    return w_in_p, wqt, wk, wvt


def _mla_layer(h, a, mod, tab, tabt, w_in, q_norm, w_uq, kv_norm, w_ukv, w_o, geom):
    w_in_p, wqt, wk, wvt = _mla_weights(w_in, w_uq, w_ukv)
    proj = _matmul(a, w_in_p, tn=640, name="mla_in")
    qt, k, vt = _mla_up(proj, q_norm, kv_norm, wqt, wk, wvt, tab, tabt)
    gated = _attention(qt, k, vt, proj, geom)
    return _matmul_residual(gated, w_o, h, mod, geom, name="mla_out")


def _gdn_layer(h, a, mod, w_in, conv_w, a_log, dt_bias, o_norm, w_o, geom):
    proj = _matmul(a, w_in, tn=1024, n_blocks=GDN_QKVZ // 1024, name="gdn_in")
    ba = _matmul(a, w_in, tn=LANES, first_block=GDN_QKVZ // LANES, n_blocks=1, name="gdn_in_gates")
    qkv = _gdn_conv(proj, conv_w, geom)
    gb, gbt = _gdn_gates(ba, a_log, dt_bias)
    o = _gdn_scan(qkv, gb, gbt, geom)
    gated = _gdn_gate(o, proj, o_norm)
    return _matmul_residual(gated, w_o, h, mod, geom, name="gdn_out")


def kernel(x, c, ctx, c_ctx, ada_w, ada_b, norm_w, mla_w_in, mla_q_norm, mla_w_uq, mla_kv_norm, mla_w_ukv, mla_w_o, gdn_w_in, gdn_conv_w, gdn_a_log, gdn_dt_bias, gdn_o_norm, gdn_w_o, final_norm):
    batch, seq, d = x.shape
    ctx_len = ctx.shape[1]
    n_rows = batch * (seq + ctx_len)
    assert d == D_MODEL and seq % ATTN_TQ == 0 and ctx_len % ROW_TILE == 0 and seq % GRID_W == 0
    assert batch < SUBLANES and n_rows % MM_TM == 0 and (batch * seq) % ctx_len == 0
    geom = Geom(batch=batch, n_lat=seq // ROW_TILE, n_ctx=ctx_len // ROW_TILE)

    h = jnp.concatenate([x.reshape(batch * seq, d), ctx.reshape(batch * ctx_len, d)], axis=0)
    conds = jnp.zeros((SUBLANES, d), F32).at[:batch].set(c).at[batch].set(c_ctx)
    mods = _adaln(conds, ada_w, ada_b)
    tab = _rope_table(batch, ctx_len, seq)
    tabt = tab.T

    for i in range(DEPTH):
        a = _prenorm(h, norm_w[i], mods[i], geom)
        j = i // 2
        if i % 2 == 0:
            h = _mla_layer(h, a, mods[i], tab, tabt, mla_w_in[j], mla_q_norm[j], mla_w_uq[j], mla_kv_norm[j],
                           mla_w_ukv[j], mla_w_o[j], geom)
        else:
            h = _gdn_layer(h, a, mods[i], gdn_w_in[j], gdn_conv_w[j], gdn_a_log[j], gdn_dt_bias[j],
                           gdn_o_norm[j], gdn_w_o[j], geom)
    return _final_norm(h, final_norm, geom).reshape(batch, seq, d)
```

```python
import functools
import math
from typing import NamedTuple

import jax
import jax.numpy as jnp
import numpy as np
from jax import lax
from jax.experimental import pallas as pl
from jax.experimental.pallas import tpu as pltpu

F32 = jnp.float32
MXU_DTYPE = jnp.bfloat16

D_MODEL = 2048
DEPTH = 4
EPS = 1e-6
GRID_W = 64
ROPE_THETA = 10000.0
MLA_HEADS = 16
MLA_NOPE = 128
MLA_ROPE = 64
MLA_V = 128
MLA_QK = MLA_NOPE + MLA_ROPE
Q_LORA = 512
KV_LORA = 512
MLA_WIDTH = MLA_HEADS * MLA_V
MLA_HEAD_COLS = 256
MLA_VT_ROWS = MLA_V + 16
MLA_IN_PAD = Q_LORA + KV_LORA + MLA_WIDTH + 2 * MLA_ROPE
GDN_K_HEADS = 16
GDN_V_HEADS = 32
GDN_DK = 128
GDN_DV = 128
GDN_KEY = GDN_K_HEADS * GDN_DK
GDN_VAL = GDN_V_HEADS * GDN_DV
GDN_CONV_CH = 2 * GDN_KEY + GDN_VAL
GDN_QKVZ = GDN_CONV_CH + GDN_VAL
CONV_W = 5

LANES = 128
SUBLANES = 8
ROW_TILE = 256
MM_TM = 512
ATTN_TQ = 1024
ATTN_TK = 256
GDN_SCAN_K_HEADS = 4
GDN_SOLVE_BLOCK = 64
VMEM_LIMIT_BYTES = 50 * 1024 * 1024


class Geom(NamedTuple):
    batch: int
    n_lat: int
    n_ctx: int

    @property
    def lat_tiles(self):
        return self.batch * self.n_lat

    def mod_row(self, tile):
        return jnp.where(tile < self.lat_tiles, tile // self.n_lat, self.batch)

    def seq_pos(self, tile):
        is_lat = tile < self.lat_tiles
        pos = jnp.where(is_lat, tile % self.n_lat, (tile - self.lat_tiles) % self.n_ctx)
        return pos, jnp.where(is_lat, self.n_lat, self.n_ctx)


def _params(*sem):
    return pltpu.CompilerParams(dimension_semantics=sem, vmem_limit_bytes=VMEM_LIMIT_BYTES)


def _sigmoid(x):
    return 1.0 / (1.0 + jnp.exp(-x))


def _silu(x):
    return x * _sigmoid(x)


def _nt_dot(a, b):
    return lax.dot_general(a, b, (((1,), (1,)), ((), ())), preferred_element_type=F32)


def _adaln_kernel(c_ref, w_ref, b_ref, o_ref):
    a = _silu(c_ref[...]).astype(MXU_DTYPE)
    w = w_ref[0].astype(MXU_DTYPE)
    o_ref[0] = jnp.dot(a, w, preferred_element_type=F32) + b_ref[0]


def _adaln(conds, ada_w, ada_b):
    depth, d, n = ada_w.shape
    tn = 768
    return pl.pallas_call(
        _adaln_kernel,
        grid=(depth, n // tn),
        in_specs=[pl.BlockSpec((SUBLANES, d), lambda l, j: (0, 0)),
                  pl.BlockSpec((1, d, tn), lambda l, j: (l, 0, j)),
                  pl.BlockSpec((1, 1, tn), lambda l, j: (l, 0, j))],
        out_specs=pl.BlockSpec((1, SUBLANES, tn), lambda l, j: (l, 0, j)),
        out_shape=jax.ShapeDtypeStruct((depth, SUBLANES, n), F32),
        compiler_params=_params("arbitrary", "arbitrary"),
        name="adaln",
    )(conds, ada_w, ada_b.reshape(depth, 1, n))


def _prenorm_kernel(x_ref, nw_ref, mod_ref, o_ref, *, geom):
    m = mod_ref[pl.ds(geom.mod_row(pl.program_id(0)), 1), :]
    x = x_ref[...]
    y = x * lax.rsqrt(jnp.mean(x * x, axis=-1, keepdims=True) + EPS) * nw_ref[...]
    o_ref[...] = (y * (1.0 + m[:, D_MODEL:2 * D_MODEL]) + m[:, :D_MODEL]).astype(o_ref.dtype)


def _prenorm(h, norm_w, mod, geom):
    t, d = h.shape
    return pl.pallas_call(
        functools.partial(_prenorm_kernel, geom=geom),
        grid=(t // ROW_TILE,),
        in_specs=[pl.BlockSpec((ROW_TILE, d), lambda i: (i, 0)),
                  pl.BlockSpec((1, d), lambda i: (0, 0)),
                  pl.BlockSpec((SUBLANES, 3 * d), lambda i: (0, 0))],
        out_specs=pl.BlockSpec((ROW_TILE, d), lambda i: (i, 0)),
        out_shape=jax.ShapeDtypeStruct((t, d), MXU_DTYPE),
        compiler_params=_params("arbitrary"),
        name="prenorm",
    )(h, norm_w.reshape(1, d), mod)


def _mm_kernel(a_ref, w_ref, o_ref, w_mxu_ref):
    @pl.when(pl.program_id(1) == 0)
    def _():
        w_mxu_ref[...] = w_ref[0].astype(MXU_DTYPE)

    o_ref[...] = jnp.dot(a_ref[...], w_mxu_ref[...], preferred_element_type=F32).astype(o_ref.dtype)


def _matmul(a, w, layer, *, tn, first_block=0, n_blocks=None, out_dtype=F32, name="matmul"):
    m, k = a.shape
    n_blocks = w.shape[2] // tn if n_blocks is None else n_blocks
    return pl.pallas_call(
        _mm_kernel,
        grid=(n_blocks, m // MM_TM),
        in_specs=[pl.BlockSpec((MM_TM, k), lambda j, i: (i, 0)),
                  pl.BlockSpec((1, k, tn), lambda j, i: (layer, 0, first_block + j))],
        out_specs=pl.BlockSpec((MM_TM, tn), lambda j, i: (i, j)),
        out_shape=jax.ShapeDtypeStruct((m, n_blocks * tn), out_dtype),
        scratch_shapes=[pltpu.VMEM((k, tn), MXU_DTYPE)],
        compiler_params=_params("arbitrary", "arbitrary"),
        name=name,
    )(a, w)


def _mm_res_kernel(a_ref, w_ref, res_ref, gate_ref, o_ref, w_mxu_ref, *, geom):
    @pl.when(pl.program_id(1) == 0)
    def _():
        w_mxu_ref[...] = w_ref[0].astype(MXU_DTYPE)

    acc = jnp.dot(a_ref[...], w_mxu_ref[...], preferred_element_type=F32)
    for s in range(MM_TM // ROW_TILE):
        tile = pl.program_id(1) * (MM_TM // ROW_TILE) + s
        g = gate_ref[pl.ds(geom.mod_row(tile), 1), :]
        rows = slice(s * ROW_TILE, (s + 1) * ROW_TILE)
        o_ref[rows, :] = res_ref[rows, :] + g * acc[rows, :]


def _matmul_residual(a, w, layer, res, mod, geom, *, tn=512, name="matmul_res"):
    m, k = a.shape
    n = w.shape[2]
    gate_block0 = 2 * D_MODEL // tn
    return pl.pallas_call(
        functools.partial(_mm_res_kernel, geom=geom),
        grid=(n // tn, m // MM_TM),
        in_specs=[pl.BlockSpec((MM_TM, k), lambda j, i: (i, 0)),
                  pl.BlockSpec((1, k, tn), lambda j, i: (layer, 0, j)),
                  pl.BlockSpec((MM_TM, tn), lambda j, i: (i, j)),
                  pl.BlockSpec((SUBLANES, tn), lambda j, i: (0, gate_block0 + j))],
        out_specs=pl.BlockSpec((MM_TM, tn), lambda j, i: (i, j)),
        out_shape=jax.ShapeDtypeStruct((m, n), F32),
        scratch_shapes=[pltpu.VMEM((k, tn), MXU_DTYPE)],
        compiler_params=_params("arbitrary", "arbitrary"),
        name=name,
    )(a, w, res, mod)


def _mla_up_kernel(cq_ref, ckv_ref, kr_ref, qn_ref, kvn_ref, wqt_ref, wk_ref, wvt_ref, tab_ref, tabt_ref,
                   qt_ref, k_ref, vt_ref):
    def rms(x, w):
        return (x * lax.rsqrt(jnp.mean(x * x, axis=-1, keepdims=True) + EPS) * w).astype(MXU_DTYPE)

    cq = rms(cq_ref[...], qn_ref[...])
    ckv = rms(ckv_ref[...], kvn_ref[...])
    tm = cq.shape[0]
    t = kr_ref[...] * tab_ref[...]
    lane = lax.broadcasted_iota(jnp.int32, t.shape, 1)
    k_rot = jnp.where(lane < MLA_ROPE, t + pltpu.roll(t, MLA_ROPE, axis=1), 0.0).astype(k_ref.dtype)
    tabt = tabt_ref[...]
    zeros = jnp.zeros((MLA_ROPE, tm), qt_ref.dtype)
    q_scale = MLA_QK ** -0.5 * math.log2(math.e)
    for h in range(MLA_HEADS):
        q0 = h * MLA_HEAD_COLS
        qh = _nt_dot(wqt_ref[q0:q0 + MLA_HEAD_COLS, :], cq) * q_scale
        qt_ref[q0:q0 + MLA_NOPE, :] = qh[:MLA_NOPE].astype(qt_ref.dtype)
        tq = qh[MLA_NOPE:] * tabt
        qt_ref[q0 + MLA_NOPE:q0 + MLA_NOPE + MLA_ROPE, :] = (tq[:MLA_ROPE] + tq[MLA_ROPE:]).astype(qt_ref.dtype)
        qt_ref[q0 + MLA_NOPE + MLA_ROPE:q0 + MLA_HEAD_COLS, :] = zeros
        k_ref[:, q0:q0 + MLA_NOPE] = jnp.dot(ckv, wk_ref[:, h * MLA_NOPE:(h + 1) * MLA_NOPE],
                                            preferred_element_type=F32).astype(k_ref.dtype)
        k_ref[:, q0 + MLA_NOPE:q0 + MLA_HEAD_COLS] = k_rot
        v0 = h * MLA_VT_ROWS
        vt_ref[v0:v0 + MLA_V, :] = _nt_dot(wvt_ref[h * MLA_V:(h + 1) * MLA_V, :], ckv).astype(vt_ref.dtype)
        vt_ref[v0 + MLA_V:v0 + MLA_VT_ROWS, :] = jnp.ones((MLA_VT_ROWS - MLA_V, tm), vt_ref.dtype)


def _mla_up(proj, q_norm, kv_norm, wqt, wk, wvt, tab, tabt):
    t = proj.shape[0]
    tm = ROW_TILE
    nq = MLA_HEADS * MLA_HEAD_COLS
    kr_block = (Q_LORA + KV_LORA + MLA_WIDTH) // LANES
    const = lambda i: (0, 0)
    return pl.pallas_call(
        _mla_up_kernel,
        grid=(t // tm,),
        in_specs=[pl.BlockSpec((tm, Q_LORA), lambda i: (i, 0)),
                  pl.BlockSpec((tm, KV_LORA), lambda i: (i, 1)),
                  pl.BlockSpec((tm, LANES), lambda i: (i, kr_block)),
                  pl.BlockSpec((1, Q_LORA), const),
                  pl.BlockSpec((1, KV_LORA), const),
                  pl.BlockSpec(wqt.shape, const),
                  pl.BlockSpec(wk.shape, const),
                  pl.BlockSpec(wvt.shape, const),
                  pl.BlockSpec((tm, LANES), lambda i: (i, 0)),
                  pl.BlockSpec((LANES, tm), lambda i: (0, i))],
        out_specs=[pl.BlockSpec((nq, tm), lambda i: (0, i)),
                   pl.BlockSpec((tm, nq), lambda i: (i, 0)),
                   pl.BlockSpec((MLA_HEADS * MLA_VT_ROWS, tm), lambda i: (0, i))],
        out_shape=[jax.ShapeDtypeStruct((nq, t), MXU_DTYPE),
                   jax.ShapeDtypeStruct((t, nq), MXU_DTYPE),
                   jax.ShapeDtypeStruct((MLA_HEADS * MLA_VT_ROWS, t), MXU_DTYPE)],
        compiler_params=_params("arbitrary"),
        name="mla_up",
    )(proj, proj, proj, q_norm.reshape(1, -1), kv_norm.reshape(1, -1), wqt, wk, wvt, tab, tabt)


def _attn_kernel(*refs, n_segments, tk):
    qt_ref = refs[0]
    k_refs = refs[1:1 + n_segments]
    vt_refs = refs[1 + n_segments:1 + 2 * n_segments]
    z_ref, o_ref = refs[1 + 2 * n_segments:]
    qt = qt_ref[...]
    tq = qt.shape[1]
    m = jnp.full((1, tq), -jnp.inf, F32)
    acc = jnp.zeros((MLA_VT_ROWS, tq), F32)
    tiles = [(k_ref, vt_ref, j * tk) for k_ref, vt_ref in zip(k_refs, vt_refs)
             for j in range(k_ref.shape[0] // tk)]

    def scores(tile):
        k_ref, _, off = tile
        return jnp.dot(k_ref[off:off + tk, :], qt, preferred_element_type=F32)

    s_next = scores(tiles[0])
    for n, (_, vt_ref, off) in enumerate(tiles):
        s = s_next
        if n + 1 < len(tiles):
            s_next = scores(tiles[n + 1])
        m_new = jnp.maximum(m, jnp.max(s, axis=0, keepdims=True))
        alpha = jnp.exp2(m - m_new)
        p = jnp.exp2(s - m_new)
        acc = alpha * acc + jnp.dot(vt_ref[:, off:off + tk], p.astype(MXU_DTYPE), preferred_element_type=F32)
        m = m_new
    o = (acc[:MLA_V] / acc[MLA_V:MLA_V + 1]).T
    o_ref[...] = (o * _silu(z_ref[...])).astype(o_ref.dtype)


def _attention(qt, k, vt, proj, geom):
    seq, ctx_len = geom.n_lat * ROW_TILE, geom.n_ctx * ROW_TILE
    ctx_block0 = geom.batch * seq // ctx_len
    z_block0 = (Q_LORA + KV_LORA) // LANES
    k_lat = pl.BlockSpec((seq, MLA_HEAD_COLS), lambda b, h, i: (b, h))
    k_ctx = pl.BlockSpec((ctx_len, MLA_HEAD_COLS), lambda b, h, i: (ctx_block0 + b, h))
    vt_lat = pl.BlockSpec((MLA_VT_ROWS, seq), lambda b, h, i: (h, b))
    vt_ctx = pl.BlockSpec((MLA_VT_ROWS, ctx_len), lambda b, h, i: (h, ctx_block0 + b))
    sem = _params("arbitrary", "arbitrary", "arbitrary")

    tq = ATTN_TQ
    nq = seq // tq
    lat = pl.pallas_call(
        functools.partial(_attn_kernel, n_segments=2, tk=ATTN_TK),
        grid=(geom.batch, MLA_HEADS, nq),
        in_specs=[pl.BlockSpec((MLA_HEAD_COLS, tq), lambda b, h, i: (h, b * nq + i)),
                  k_ctx, k_lat, vt_ctx, vt_lat,
                  pl.BlockSpec((tq, MLA_V), lambda b, h, i: (b * nq + i, z_block0 + h))],
        out_specs=pl.BlockSpec((tq, MLA_V), lambda b, h, i: (b * nq + i, h)),
        out_shape=jax.ShapeDtypeStruct((geom.batch * seq, MLA_WIDTH), MXU_DTYPE),
        compiler_params=sem, name="mla_attention",
    )(qt, k, k, vt, vt, proj)
    ctx = pl.pallas_call(
        functools.partial(_attn_kernel, n_segments=1, tk=ATTN_TK),
        grid=(geom.batch, MLA_HEADS, 1),
        in_specs=[pl.BlockSpec((MLA_HEAD_COLS, ctx_len), lambda b, h, i: (h, ctx_block0 + b)),
                  k_ctx, vt_ctx,
                  pl.BlockSpec((ctx_len, MLA_V), lambda b, h, i: (ctx_block0 + b, z_block0 + h))],
        out_specs=pl.BlockSpec((ctx_len, MLA_V), lambda b, h, i: (b, h)),
        out_shape=jax.ShapeDtypeStruct((geom.batch * ctx_len, MLA_WIDTH), MXU_DTYPE),
        compiler_params=sem, name="mla_attention_ctx",
    )(qt, k, vt, proj)
    return jnp.concatenate([lat, ctx], axis=0)


def _gdn_conv_kernel(cur_ref, prev_ref, next_ref, w_ref, o_ref, pad_ref, *, geom, tc):
    c = pl.program_id(1)
    pos, n_tiles = geom.seq_pos(pl.program_id(0))
    pad_ref[0:SUBLANES, :] = jnp.where(pos != 0, prev_ref[...], 0.0)
    pad_ref[SUBLANES:SUBLANES + ROW_TILE, :] = cur_ref[...]
    pad_ref[SUBLANES + ROW_TILE:, :] = jnp.where(pos != n_tiles - 1, next_ref[...], 0.0)
    w = w_ref[...]
    y = jnp.zeros((ROW_TILE, tc), F32)
    for k in range(CONV_W):
        start = SUBLANES - CONV_W // 2 + k
        y = y + pad_ref[start:start + ROW_TILE, :] * w[k:k + 1, :]
    y = _silu(y)
    key_tiles = GDN_KEY // tc

    @pl.when(c < 2 * key_tiles)
    def _():
        scale = jnp.where(c < key_tiles, GDN_DK ** -0.5, 1.0)
        for g in range(tc // GDN_DK):
            yg = y[:, g * GDN_DK:(g + 1) * GDN_DK]
            inv = lax.rsqrt(jnp.sum(yg * yg, axis=-1, keepdims=True) + EPS) * scale
            o_ref[:, g * GDN_DK:(g + 1) * GDN_DK] = (yg * inv).astype(o_ref.dtype)

    @pl.when(c >= 2 * key_tiles)
    def _():
        o_ref[...] = y.astype(o_ref.dtype)


def _gdn_conv(proj, conv_w, geom):
    t = proj.shape[0]
    tc = 512
    halo_per_tile = ROW_TILE // SUBLANES
    last_halo = t // SUBLANES - 1
    return pl.pallas_call(
        functools.partial(_gdn_conv_kernel, geom=geom, tc=tc),
        grid=(t // ROW_TILE, GDN_CONV_CH // tc),
        in_specs=[pl.BlockSpec((ROW_TILE, tc), lambda i, c: (i, c)),
                  pl.BlockSpec((SUBLANES, tc), lambda i, c: (jnp.maximum(i * halo_per_tile - 1, 0), c)),
                  pl.BlockSpec((SUBLANES, tc), lambda i, c: (jnp.minimum((i + 1) * halo_per_tile, last_halo), c)),
                  pl.BlockSpec((CONV_W, tc), lambda i, c: (0, c))],
        out_specs=pl.BlockSpec((ROW_TILE, tc), lambda i, c: (i, c)),
        out_shape=jax.ShapeDtypeStruct((t, GDN_CONV_CH), MXU_DTYPE),
        scratch_shapes=[pltpu.VMEM((ROW_TILE + 2 * SUBLANES, tc), F32)],
        compiler_params=_params("arbitrary", "arbitrary"),
        name="gdn_conv",
    )(proj, proj, proj, conv_w)


def _split3(x):
    hi = x.astype(MXU_DTYPE)
    r = x - hi.astype(F32)
    mid = r.astype(MXU_DTYPE)
    lo = (r - mid.astype(F32)).astype(MXU_DTYPE)
    return hi, mid, lo


def _gdn_gates_kernel(ba_ref, alog_ref, dtb_ref, gb_ref, gbt_ref):
    half = LANES // 2
    ba = ba_ref[...]
    beta = _sigmoid(ba[:, :half])
    x = ba[:, half:] + dtb_ref[...]
    softplus = jnp.maximum(x, 0.0) + jnp.log(1.0 + jnp.exp(-jnp.abs(x)))
    g = -jnp.exp(alog_ref[...]) * softplus
    row = lax.broadcasted_iota(jnp.int32, (ROW_TILE, ROW_TILE), 0)
    col = lax.broadcasted_iota(jnp.int32, (ROW_TILE, ROW_TILE), 1)
    lower = (row >= col).astype(MXU_DTYPE)
    upper = (row <= col).astype(MXU_DTYPE)
    pre = jnp.zeros((ROW_TILE, half), F32)
    suf = jnp.zeros((ROW_TILE, half), F32)
    for piece in _split3(g):
        pre = pre + jnp.dot(lower, piece, preferred_element_type=F32)
        suf = suf + jnp.dot(upper, piece, preferred_element_type=F32)
    lane = lax.broadcasted_iota(jnp.int32, (ROW_TILE, half), 1)
    gcum = jnp.where(lane < half // 2, pre, suf)
    gb = jnp.concatenate([beta, gcum], axis=-1)
    gb_ref[...] = gb
    gbt_ref[...] = gb.T


def _gdn_gates(ba, a_log, dt_bias):
    t = ba.shape[0]
    half = LANES // 2
    return pl.pallas_call(
        _gdn_gates_kernel,
        grid=(t // ROW_TILE,),
        in_specs=[pl.BlockSpec((ROW_TILE, LANES), lambda i: (i, 0)),
                  pl.BlockSpec((1, half), lambda i: (0, 0)),
                  pl.BlockSpec((1, half), lambda i: (0, 0))],
        out_specs=[pl.BlockSpec((ROW_TILE, LANES), lambda i: (i, 0)),
                   pl.BlockSpec((LANES, ROW_TILE), lambda i: (0, i))],
        out_shape=[jax.ShapeDtypeStruct((t, LANES), F32), jax.ShapeDtypeStruct((LANES, t), F32)],
        compiler_params=_params("arbitrary"),
        name="gdn_gates",
    )(ba, a_log.reshape(1, half), dt_bias.reshape(1, half))


def _mxu_dot(a, b):
    return jnp.dot(a.astype(MXU_DTYPE), b.astype(MXU_DTYPE), preferred_element_type=F32)


def _gdn_scan_kernel(q_ref, k_ref, v_ref, gb_ref, gbt_ref, o_ref, state_ref):
    n = ROW_TILE
    rep = GDN_V_HEADS // GDN_K_HEADS
    d = pl.program_id(2)

    @pl.when(pl.program_id(3) == 0)
    def _():
        state_ref[...] = jnp.zeros(state_ref.shape, F32)

    row = lax.broadcasted_iota(jnp.int32, (n, n), 0)
    col = lax.broadcasted_iota(jnp.int32, (n, n), 1)
    fwd = d == 0
    ahead = (row - col) * jnp.where(fwd, 1, -1)
    incl = ahead >= 0
    strict = ahead > 0
    block_xor = row ^ col
    last = jnp.where(fwd, n - 1, 0)
    lane = lax.broadcasted_iota(jnp.int32, (n, LANES), 1)
    sub1 = lax.broadcasted_iota(jnp.int32, (n, 1), 0)
    gb = gb_ref[...]
    half = LANES // 2

    probs = []
    for kh in range(GDN_SCAN_K_HEADS):
        q = q_ref[:, kh * GDN_DK:(kh + 1) * GDN_DK]
        k = k_ref[:, kh * GDN_DK:(kh + 1) * GDN_DK]
        gram = _nt_dot(jnp.concatenate([q, k], axis=0), k)
        qk, kk = gram[:n], gram[n:]
        for e in range(rep):
            idx = kh * rep + e
            c_beta = d * GDN_V_HEADS + (pl.program_id(1) * GDN_SCAN_K_HEADS + kh) * rep + e
            c_g = half + c_beta
            beta_c = jnp.sum(jnp.where(lane == c_beta, gb, 0.0), axis=-1, keepdims=True)
            g_c = jnp.sum(jnp.where(lane == c_g, gb, 0.0), axis=-1, keepdims=True)
            beta_r = gbt_ref[pl.ds(c_beta, 1), :]
            g_r = gbt_ref[pl.ds(c_g, 1), :]
            g_last = jnp.sum(jnp.where(sub1 == last, g_c, 0.0), axis=0, keepdims=True)
            decay = jnp.exp(jnp.where(incl, g_c - g_r, -jnp.inf))
            probs.append(dict(
                idx=idx, q=q, k=k, v=v_ref[:, idx * GDN_DV:(idx + 1) * GDN_DV], g_c=g_c, g_last=g_last,
                beta_r=beta_r, qk_dec=qk * decay,
                a_mat=jnp.where(strict, beta_c * kk * decay, 0.0)))

    c = GDN_SOLVE_BLOCK
    log_c = int(np.log2(c))
    p_row = lax.broadcasted_iota(jnp.int32, (c, n), 0)
    p_lane = lax.broadcasted_iota(jnp.int32, (c, n), 1)
    p_blk = p_lane >> log_c
    p_xor = p_row ^ (p_lane & (c - 1))

    def pack(mat):
        out = mat[n - c:]
        for a in range(n // c - 2, -1, -1):
            out = jnp.where(p_blk == a, mat[a * c:(a + 1) * c], out)
        return out

    def unpack(packed):
        return jnp.concatenate([jnp.where(p_blk == a, packed, 0.0) for a in range(n // c)], axis=0)

    packed_a = [pack(p["a_mat"]) for p in probs]
    packed_t = [(p_xor == 0).astype(F32) - jnp.where(p_xor == 1, pa, 0.0) for pa in packed_a]
    for log_m in range(1, log_c):
        off = (p_xor >> log_m) == 1
        xs = [_mxu_dot(jnp.where(off, pa, 0.0), unpack(pt)) for pa, pt in zip(packed_a, packed_t)]
        packed_t = [pt - _mxu_dot(pt, unpack(x)) for pt, x in zip(packed_t, xs)]
    invs = [unpack(pt) for pt in packed_t]
    for log_m in range(log_c, int(np.log2(n))):
        off = (block_xor >> log_m) == 1
        xs = [_mxu_dot(jnp.where(off, p["a_mat"], 0.0), inv) for p, inv in zip(probs, invs)]
        invs = [inv - _mxu_dot(inv, x) for inv, x in zip(invs, xs)]

    egs = [jnp.exp(p["g_c"]) for p in probs]
    uws = [_mxu_dot(inv * p["beta_r"], jnp.concatenate([p["v"].astype(F32), p["k"].astype(F32) * eg], axis=1))
           for p, inv, eg in zip(probs, invs, egs)]
    us = [uw[:, :GDN_DV] for uw in uws]
    ws = [uw[:, GDN_DV:] for uw in uws]
    states = [state_ref[p["idx"]] for p in probs]
    q_decs = [p["q"].astype(F32) * eg for p, eg in zip(probs, egs)]
    from_state = [_mxu_dot(jnp.concatenate([w, q_dec], axis=0), s) for w, q_dec, s in zip(ws, q_decs, states)]
    v_news = [u - r[:n] for u, r in zip(us, from_state)]
    outs = [r[n:] + _mxu_dot(p["qk_dec"], v_new) for p, r, v_new in zip(probs, from_state, v_news)]
    for p, s, v_new, o in zip(probs, states, v_news, outs):
        i = p["idx"]
        o_ref[0, :, i * GDN_DV:(i + 1) * GDN_DV] = o
        k_dec = p["k"].astype(F32) * jnp.exp(p["g_last"] - p["g_c"])
        state_ref[i] = s * jnp.exp(p["g_last"]) + lax.dot_general(
            k_dec.astype(MXU_DTYPE), v_new.astype(MXU_DTYPE), (((0,), (0,)), ((), ())),
            preferred_element_type=F32)


def _gdn_scan(qkv, gb, gbt, geom):
    t = qkv.shape[0]
    rep = GDN_V_HEADS // GDN_K_HEADS
    kw = GDN_SCAN_K_HEADS * GDN_DK
    vw = GDN_SCAN_K_HEADS * rep * GDN_DV

    def tile(b, d, i):
        ctx_pos = jnp.where(d == 0, i, geom.n_ctx - 1 - i)
        lat_pos = jnp.where(d == 0, i - geom.n_ctx, geom.n_lat - 1 - (i - geom.n_ctx))
        return jnp.where(i < geom.n_ctx, geom.lat_tiles + b * geom.n_ctx + ctx_pos, b * geom.n_lat + lat_pos)

    return pl.pallas_call(
        _gdn_scan_kernel,
        grid=(geom.batch, GDN_K_HEADS // GDN_SCAN_K_HEADS, 2, geom.n_ctx + geom.n_lat),
        in_specs=[pl.BlockSpec((ROW_TILE, kw), lambda b, h, d, i: (tile(b, d, i), h)),
                  pl.BlockSpec((ROW_TILE, kw), lambda b, h, d, i: (tile(b, d, i), GDN_KEY // kw + h)),
                  pl.BlockSpec((ROW_TILE, vw), lambda b, h, d, i: (tile(b, d, i), 2 * GDN_KEY // vw + h)),
                  pl.BlockSpec((ROW_TILE, LANES), lambda b, h, d, i: (tile(b, d, i), 0)),
                  pl.BlockSpec((LANES, ROW_TILE), lambda b, h, d, i: (0, tile(b, d, i)))],
        out_specs=pl.BlockSpec((1, ROW_TILE, vw), lambda b, h, d, i: (d, tile(b, d, i), h)),
        out_shape=jax.ShapeDtypeStruct((2, t, GDN_VAL), F32),
        scratch_shapes=[pltpu.VMEM((GDN_SCAN_K_HEADS * rep, GDN_DK, GDN_DV), F32)],
        compiler_params=_params("arbitrary", "arbitrary", "arbitrary", "arbitrary"),
        name="gdn_scan",
    )(qkv, qkv, qkv, gb, gbt)


def _gdn_gate_kernel(o_ref, z_ref, nw_ref, a_ref):
    o = o_ref[0] + o_ref[1]
    for g in range(o.shape[1] // GDN_DV):
        cols = slice(g * GDN_DV, (g + 1) * GDN_DV)
        og = o[:, cols]
        y = og * lax.rsqrt(jnp.mean(og * og, axis=-1, keepdims=True) + EPS) * nw_ref[...]
        a_ref[:, cols] = (y * _silu(z_ref[:, cols])).astype(a_ref.dtype)


def _gdn_gate(o, proj, o_norm):
    t = o.shape[1]
    tc = 512
    z_block0 = GDN_CONV_CH // tc
    return pl.pallas_call(
        _gdn_gate_kernel,
        grid=(t // ROW_TILE, GDN_VAL // tc),
        in_specs=[pl.BlockSpec((2, ROW_TILE, tc), lambda i, c: (0, i, c)),
                  pl.BlockSpec((ROW_TILE, tc), lambda i, c: (i, z_block0 + c)),
                  pl.BlockSpec((1, GDN_DV), lambda i, c: (0, 0))],
        out_specs=pl.BlockSpec((ROW_TILE, tc), lambda i, c: (i, c)),
        out_shape=jax.ShapeDtypeStruct((t, GDN_VAL), MXU_DTYPE),
        compiler_params=_params("arbitrary", "arbitrary"),
        name="gdn_gate",
    )(o, proj, o_norm.reshape(1, GDN_DV))


def _final_norm_kernel(x_ref, w_ref, o_ref):
    x = x_ref[...]
    o_ref[...] = x * lax.rsqrt(jnp.mean(x * x, axis=-1, keepdims=True) + EPS) * w_ref[...]


def _final_norm(h, w, geom):
    d = h.shape[1]
    return pl.pallas_call(
        _final_norm_kernel,
        grid=(geom.lat_tiles,),
        in_specs=[pl.BlockSpec((ROW_TILE, d), lambda i: (i, 0)),
                  pl.BlockSpec((1, d), lambda i: (0, 0))],
        out_specs=pl.BlockSpec((ROW_TILE, d), lambda i: (i, 0)),
        out_shape=jax.ShapeDtypeStruct((geom.lat_tiles * ROW_TILE, d), F32),
        compiler_params=_params("arbitrary"),
        name="final_norm",
    )(h, w.reshape(1, d))


def _rope_table(batch, ctx_len, seq):
    pos = jnp.arange(seq)
    r = (pos // GRID_W).astype(F32)
    col = (pos % GRID_W).astype(F32)
    half = MLA_ROPE // 2
    inv = ROPE_THETA ** (-jnp.arange(0, half, 2, dtype=F32) / half)
    ang = jnp.concatenate([r[:, None] * inv, col[:, None] * inv], axis=-1)
    cos, sin = jnp.cos(ang), jnp.sin(ang)
    lat = jnp.concatenate([cos, cos, -sin, sin], axis=-1)
    ident = jnp.concatenate([jnp.ones((ctx_len, MLA_ROPE), F32), jnp.zeros((ctx_len, MLA_ROPE), F32)], axis=-1)
    return jnp.concatenate([jnp.tile(lat, (batch, 1)), jnp.tile(ident, (batch, 1))], axis=0)


_ROPE_SWAP = np.concatenate([np.arange(MLA_ROPE // 2, MLA_ROPE), np.arange(MLA_ROPE // 2)])


def _mla_weights(w_in, w_uq, w_ukv):
    lo = Q_LORA + KV_LORA
    k_rope = w_in[:, lo:lo + MLA_ROPE]
    w_in_p = jnp.concatenate([w_in[:, :lo], w_in[:, lo + MLA_ROPE:], k_rope, k_rope[:, _ROPE_SWAP]], axis=1)
    head = np.concatenate([np.arange(MLA_QK), MLA_NOPE + _ROPE_SWAP])
    cols = (np.arange(MLA_HEADS)[:, None] * MLA_QK + head[None, :]).reshape(-1)
    wqt = w_uq[:, cols].T.astype(MXU_DTYPE)
    w_kv = w_ukv.reshape(KV_LORA, MLA_HEADS, MLA_NOPE + MLA_V)
    wk = w_kv[:, :, :MLA_NOPE].reshape(KV_LORA, MLA_HEADS * MLA_NOPE).astype(MXU_DTYPE)
    wvt = w_kv[:, :, MLA_NOPE:].reshape(KV_LORA, MLA_WIDTH).T.astype(MXU_DTYPE)
    return w_in_p, wqt, wk, wvt


def _mla_layer(h, a, mod, tab, tabt, w_in, q_norm, w_uq, kv_norm, w_ukv, w_o_all, layer, geom):
    w_in_p, wqt, wk, wvt = _mla_weights(w_in, w_uq, w_ukv)
    proj = _matmul(a, w_in_p[None], 0, tn=640, name="mla_in")
    qt, k, vt = _mla_up(proj, q_norm, kv_norm, wqt, wk, wvt, tab, tabt)
    gated = _attention(qt, k, vt, proj, geom)
    return _matmul_residual(gated, w_o_all, layer, h, mod, geom, name="mla_out")


def _gdn_layer(h, a, mod, w_in_all, conv_w, a_log, dt_bias, o_norm, w_o_all, layer, geom):
    proj = _matmul(a, w_in_all, layer, tn=1024, n_blocks=GDN_QKVZ // 1024, name="gdn_in")
    ba = _matmul(a, w_in_all, layer, tn=LANES, first_block=GDN_QKVZ // LANES, n_blocks=1, name="gdn_in_gates")
    qkv = _gdn_conv(proj, conv_w, geom)
    gb, gbt = _gdn_gates(ba, a_log, dt_bias)
    o = _gdn_scan(qkv, gb, gbt, geom)
    gated = _gdn_gate(o, proj, o_norm)
    return _matmul_residual(gated, w_o_all, layer, h, mod, geom, name="gdn_out")


def kernel(x, c, ctx, c_ctx, ada_w, ada_b, norm_w, mla_w_in, mla_q_norm, mla_w_uq, mla_kv_norm, mla_w_ukv, mla_w_o, gdn_w_in, gdn_conv_w, gdn_a_log, gdn_dt_bias, gdn_o_norm, gdn_w_o, final_norm):
    batch, seq, d = x.shape
    ctx_len = ctx.shape[1]
    n_rows = batch * (seq + ctx_len)
    assert d == D_MODEL and seq % ATTN_TQ == 0 and ctx_len % ROW_TILE == 0 and seq % GRID_W == 0
    assert batch < SUBLANES and n_rows % MM_TM == 0 and (batch * seq) % ctx_len == 0
    geom = Geom(batch=batch, n_lat=seq // ROW_TILE, n_ctx=ctx_len // ROW_TILE)

    h = jnp.concatenate([x.reshape(batch * seq, d), ctx.reshape(batch * ctx_len, d)], axis=0)
    conds = jnp.zeros((SUBLANES, d), F32).at[:batch].set(c).at[batch].set(c_ctx)
    mods = _adaln(conds, ada_w, ada_b)
    tab = _rope_table(batch, ctx_len, seq)
    tabt = tab.T

    for i in range(DEPTH):
        a = _prenorm(h, norm_w[i], mods[i], geom)
        j = i // 2
        if i % 2 == 0:
            h = _mla_layer(h, a, mods[i], tab, tabt, mla_w_in[j], mla_q_norm[j], mla_w_uq[j], mla_kv_norm[j],
                           mla_w_ukv[j], mla_w_o, j, geom)
        else:
            h = _gdn_layer(h, a, mods[i], gdn_w_in, gdn_conv_w[j], gdn_a_log[j], gdn_dt_bias[j],
                           gdn_o_norm[j], gdn_w_o, j, geom)
    return _final_norm(h, final_norm, geom).reshape(batch, seq, d)
```

```python
import functools
import math
from typing import NamedTuple

import jax
import jax.numpy as jnp
import numpy as np
from jax import lax
from jax.experimental import pallas as pl
from jax.experimental.pallas import tpu as pltpu

F32 = jnp.float32
MXU_DTYPE = jnp.bfloat16

D_MODEL = 2048
DEPTH = 4
EPS = 1e-6
GRID_W = 64
ROPE_THETA = 10000.0
MLA_HEADS = 16
MLA_NOPE = 128
MLA_ROPE = 64
MLA_V = 128
MLA_QK = MLA_NOPE + MLA_ROPE
Q_LORA = 512
KV_LORA = 512
MLA_WIDTH = MLA_HEADS * MLA_V
MLA_HEAD_COLS = 256
MLA_VT_ROWS = MLA_V + 16
MLA_IN_PAD = Q_LORA + KV_LORA + MLA_WIDTH + 2 * MLA_ROPE
GDN_K_HEADS = 16
GDN_V_HEADS = 32
GDN_DK = 128
GDN_DV = 128
GDN_KEY = GDN_K_HEADS * GDN_DK
GDN_VAL = GDN_V_HEADS * GDN_DV
GDN_CONV_CH = 2 * GDN_KEY + GDN_VAL
GDN_QKVZ = GDN_CONV_CH + GDN_VAL
CONV_W = 5

LANES = 128
SUBLANES = 8
ROW_TILE = 256
MM_TM = 512
ATTN_TQ = 1024
ATTN_TK = 256
GDN_SCAN_K_HEADS = 4
GDN_SOLVE_BLOCK = 64
VMEM_LIMIT_BYTES = 50 * 1024 * 1024


class Geom(NamedTuple):
    batch: int
    n_lat: int
    n_ctx: int

    @property
    def lat_tiles(self):
        return self.batch * self.n_lat

    def mod_row(self, tile):
        return jnp.where(tile < self.lat_tiles, tile // self.n_lat, self.batch)

    def seq_pos(self, tile):
        is_lat = tile < self.lat_tiles
        pos = jnp.where(is_lat, tile % self.n_lat, (tile - self.lat_tiles) % self.n_ctx)
        return pos, jnp.where(is_lat, self.n_lat, self.n_ctx)


def _params(*sem):
    return pltpu.CompilerParams(dimension_semantics=sem, vmem_limit_bytes=VMEM_LIMIT_BYTES)


def _sigmoid(x):
    return 1.0 / (1.0 + jnp.exp(-x))


def _silu(x):
    return x * _sigmoid(x)


def _nt_dot(a, b):
    return lax.dot_general(a, b, (((1,), (1,)), ((), ())), preferred_element_type=F32)


def _adaln_kernel(c_ref, w_ref, b_ref, o_ref):
    a = _silu(c_ref[...]).astype(MXU_DTYPE)
    w = w_ref[0].astype(MXU_DTYPE)
    o_ref[0] = jnp.dot(a, w, preferred_element_type=F32) + b_ref[0]


def _adaln(conds, ada_w, ada_b):
    depth, d, n = ada_w.shape
    tn = 768
    return pl.pallas_call(
        _adaln_kernel,
        grid=(depth, n // tn),
        in_specs=[pl.BlockSpec((SUBLANES, d), lambda l, j: (0, 0)),
                  pl.BlockSpec((1, d, tn), lambda l, j: (l, 0, j)),
                  pl.BlockSpec((1, 1, tn), lambda l, j: (l, 0, j))],
        out_specs=pl.BlockSpec((1, SUBLANES, tn), lambda l, j: (l, 0, j)),
        out_shape=jax.ShapeDtypeStruct((depth, SUBLANES, n), F32),
        compiler_params=_params("arbitrary", "arbitrary"),
        name="adaln",
    )(conds, ada_w, ada_b.reshape(depth, 1, n))


def _prenorm_kernel(x_ref, nw_ref, mod_ref, o_ref, *, geom):
    m = mod_ref[pl.ds(geom.mod_row(pl.program_id(0)), 1), :]
    x = x_ref[...]
    y = x * lax.rsqrt(jnp.mean(x * x, axis=-1, keepdims=True) + EPS) * nw_ref[...]
    o_ref[...] = (y * (1.0 + m[:, D_MODEL:2 * D_MODEL]) + m[:, :D_MODEL]).astype(o_ref.dtype)


def _prenorm(h, norm_w, mod, geom):
    t, d = h.shape
    return pl.pallas_call(
        functools.partial(_prenorm_kernel, geom=geom),
        grid=(t // ROW_TILE,),
        in_specs=[pl.BlockSpec((ROW_TILE, d), lambda i: (i, 0)),
                  pl.BlockSpec((1, d), lambda i: (0, 0)),
                  pl.BlockSpec((SUBLANES, 3 * d), lambda i: (0, 0))],
        out_specs=pl.BlockSpec((ROW_TILE, d), lambda i: (i, 0)),
        out_shape=jax.ShapeDtypeStruct((t, d), MXU_DTYPE),
        compiler_params=_params("arbitrary"),
        name="prenorm",
    )(h, norm_w.reshape(1, d), mod)


def _mm_kernel(a_ref, w_ref, o_ref, w_mxu_ref):
    @pl.when(pl.program_id(1) == 0)
    def _():
        w_mxu_ref[...] = w_ref[0].astype(MXU_DTYPE)

    o_ref[...] = jnp.dot(a_ref[...], w_mxu_ref[...], preferred_element_type=F32).astype(o_ref.dtype)


def _matmul(a, w, layer, *, tn, first_block=0, n_blocks=None, out_dtype=F32, name="matmul"):
    m, k = a.shape
    n_blocks = w.shape[2] // tn if n_blocks is None else n_blocks
    return pl.pallas_call(
        _mm_kernel,
        grid=(n_blocks, m // MM_TM),
        in_specs=[pl.BlockSpec((MM_TM, k), lambda j, i: (i, 0)),
                  pl.BlockSpec((1, k, tn), lambda j, i: (layer, 0, first_block + j))],
        out_specs=pl.BlockSpec((MM_TM, tn), lambda j, i: (i, j)),
        out_shape=jax.ShapeDtypeStruct((m, n_blocks * tn), out_dtype),
        scratch_shapes=[pltpu.VMEM((k, tn), MXU_DTYPE)],
        compiler_params=_params("arbitrary", "arbitrary"),
        name=name,
    )(a, w)


def _mm_res_kernel(a_ref, w_ref, res_ref, gate_ref, o_ref, w_mxu_ref, *, geom):
    @pl.when(pl.program_id(1) == 0)
    def _():
        w_mxu_ref[...] = w_ref[0].astype(MXU_DTYPE)

    acc = jnp.dot(a_ref[...], w_mxu_ref[...], preferred_element_type=F32)
    for s in range(MM_TM // ROW_TILE):
        tile = pl.program_id(1) * (MM_TM // ROW_TILE) + s
        g = gate_ref[pl.ds(geom.mod_row(tile), 1), :]
        rows = slice(s * ROW_TILE, (s + 1) * ROW_TILE)
        o_ref[rows, :] = res_ref[rows, :] + g * acc[rows, :]


def _matmul_residual(a, w, layer, res, mod, geom, *, tn=512, name="matmul_res"):
    m, k = a.shape
    n = w.shape[2]
    gate_block0 = 2 * D_MODEL // tn
    return pl.pallas_call(
        functools.partial(_mm_res_kernel, geom=geom),
        grid=(n // tn, m // MM_TM),
        in_specs=[pl.BlockSpec((MM_TM, k), lambda j, i: (i, 0)),
                  pl.BlockSpec((1, k, tn), lambda j, i: (layer, 0, j)),
                  pl.BlockSpec((MM_TM, tn), lambda j, i: (i, j)),
                  pl.BlockSpec((SUBLANES, tn), lambda j, i: (0, gate_block0 + j))],
        out_specs=pl.BlockSpec((MM_TM, tn), lambda j, i: (i, j)),
        out_shape=jax.ShapeDtypeStruct((m, n), F32),
        scratch_shapes=[pltpu.VMEM((k, tn), MXU_DTYPE)],
        compiler_params=_params("arbitrary", "arbitrary"),
        name=name,
    )(a, w, res, mod)


def _mla_up_kernel(cq_ref, ckv_ref, kr_ref, qn_ref, kvn_ref, wqt_ref, wk_ref, wvt_ref, tab_ref, tabt_ref,
                   qt_ref, k_ref, vt_ref):
    def rms(x, w):
        return (x * lax.rsqrt(jnp.mean(x * x, axis=-1, keepdims=True) + EPS) * w).astype(MXU_DTYPE)

    cq = rms(cq_ref[...], qn_ref[...])
    ckv = rms(ckv_ref[...], kvn_ref[...])
    tm = cq.shape[0]
    t = kr_ref[...] * tab_ref[...]
    lane = lax.broadcasted_iota(jnp.int32, t.shape, 1)
    k_rot = jnp.where(lane < MLA_ROPE, t + pltpu.roll(t, MLA_ROPE, axis=1), 0.0).astype(k_ref.dtype)
    tabt = tabt_ref[...]
    zeros = jnp.zeros((MLA_ROPE, tm), qt_ref.dtype)
    q_scale = MLA_QK ** -0.5 * math.log2(math.e)
    for h in range(MLA_HEADS):
        q0 = h * MLA_HEAD_COLS
        qh = _nt_dot(wqt_ref[q0:q0 + MLA_HEAD_COLS, :], cq) * q_scale
        qt_ref[q0:q0 + MLA_NOPE, :] = qh[:MLA_NOPE].astype(qt_ref.dtype)
        tq = qh[MLA_NOPE:] * tabt
        qt_ref[q0 + MLA_NOPE:q0 + MLA_NOPE + MLA_ROPE, :] = (tq[:MLA_ROPE] + tq[MLA_ROPE:]).astype(qt_ref.dtype)
        qt_ref[q0 + MLA_NOPE + MLA_ROPE:q0 + MLA_HEAD_COLS, :] = zeros
        k_ref[:, q0:q0 + MLA_NOPE] = jnp.dot(ckv, wk_ref[:, h * MLA_NOPE:(h + 1) * MLA_NOPE],
                                            preferred_element_type=F32).astype(k_ref.dtype)
        k_ref[:, q0 + MLA_NOPE:q0 + MLA_HEAD_COLS] = k_rot
        v0 = h * MLA_VT_ROWS
        vt_ref[v0:v0 + MLA_V, :] = _nt_dot(wvt_ref[h * MLA_V:(h + 1) * MLA_V, :], ckv).astype(vt_ref.dtype)
        vt_ref[v0 + MLA_V:v0 + MLA_VT_ROWS, :] = jnp.ones((MLA_VT_ROWS - MLA_V, tm), vt_ref.dtype)


def _mla_up(proj, q_norm, kv_norm, wqt, wk, wvt, tab, tabt):
    t = proj.shape[0]
    tm = ROW_TILE
    nq = MLA_HEADS * MLA_HEAD_COLS
    kr_block = (Q_LORA + KV_LORA + MLA_WIDTH) // LANES
    const = lambda i: (0, 0)
    return pl.pallas_call(
        _mla_up_kernel,
        grid=(t // tm,),
        in_specs=[pl.BlockSpec((tm, Q_LORA), lambda i: (i, 0)),
                  pl.BlockSpec((tm, KV_LORA), lambda i: (i, 1)),
                  pl.BlockSpec((tm, LANES), lambda i: (i, kr_block)),
                  pl.BlockSpec((1, Q_LORA), const),
                  pl.BlockSpec((1, KV_LORA), const),
                  pl.BlockSpec(wqt.shape, const),
                  pl.BlockSpec(wk.shape, const),
                  pl.BlockSpec(wvt.shape, const),
                  pl.BlockSpec((tm, LANES), lambda i: (i, 0)),
                  pl.BlockSpec((LANES, tm), lambda i: (0, i))],
        out_specs=[pl.BlockSpec((nq, tm), lambda i: (0, i)),
                   pl.BlockSpec((tm, nq), lambda i: (i, 0)),
                   pl.BlockSpec((MLA_HEADS * MLA_VT_ROWS, tm), lambda i: (0, i))],
        out_shape=[jax.ShapeDtypeStruct((nq, t), MXU_DTYPE),
                   jax.ShapeDtypeStruct((t, nq), MXU_DTYPE),
                   jax.ShapeDtypeStruct((MLA_HEADS * MLA_VT_ROWS, t), MXU_DTYPE)],
        compiler_params=_params("arbitrary"),
        name="mla_up",
    )(proj, proj, proj, q_norm.reshape(1, -1), kv_norm.reshape(1, -1), wqt, wk, wvt, tab, tabt)


def _attn_kernel(*refs, n_segments, tk):
    qt_ref = refs[0]
    k_refs = refs[1:1 + n_segments]
    vt_refs = refs[1 + n_segments:1 + 2 * n_segments]
    z_ref, o_ref = refs[1 + 2 * n_segments:]
    qt = qt_ref[...]
    tq = qt.shape[1]
    m = jnp.full((1, tq), -jnp.inf, F32)
    acc = jnp.zeros((MLA_VT_ROWS, tq), F32)
    tiles = [(k_ref, vt_ref, j * tk) for k_ref, vt_ref in zip(k_refs, vt_refs)
             for j in range(k_ref.shape[0] // tk)]

    def scores(tile):
        k_ref, _, off = tile
        return jnp.dot(k_ref[off:off + tk, :], qt, preferred_element_type=F32)

    s_next = scores(tiles[0])
    for n, (_, vt_ref, off) in enumerate(tiles):
        s = s_next
        if n + 1 < len(tiles):
            s_next = scores(tiles[n + 1])
        m_new = jnp.maximum(m, jnp.max(s, axis=0, keepdims=True))
        alpha = jnp.exp2(m - m_new)
        p = jnp.exp2(s - m_new)
        acc = alpha * acc + jnp.dot(vt_ref[:, off:off + tk], p.astype(MXU_DTYPE), preferred_element_type=F32)
        m = m_new
    o = (acc[:MLA_V] / acc[MLA_V:MLA_V + 1]).T
    o_ref[...] = (o * _silu(z_ref[...])).astype(o_ref.dtype)


def _attention(qt, k, vt, proj, geom):
    seq, ctx_len = geom.n_lat * ROW_TILE, geom.n_ctx * ROW_TILE
    ctx_block0 = geom.batch * seq // ctx_len
    z_block0 = (Q_LORA + KV_LORA) // LANES
    k_lat = pl.BlockSpec((seq, MLA_HEAD_COLS), lambda b, h, i: (b, h))
    k_ctx = pl.BlockSpec((ctx_len, MLA_HEAD_COLS), lambda b, h, i: (ctx_block0 + b, h))
    vt_lat = pl.BlockSpec((MLA_VT_ROWS, seq), lambda b, h, i: (h, b))
    vt_ctx = pl.BlockSpec((MLA_VT_ROWS, ctx_len), lambda b, h, i: (h, ctx_block0 + b))
    sem = _params("arbitrary", "arbitrary", "arbitrary")

    tq = ATTN_TQ
    nq = seq // tq
    lat = pl.pallas_call(
        functools.partial(_attn_kernel, n_segments=2, tk=ATTN_TK),
        grid=(geom.batch, MLA_HEADS, nq),
        in_specs=[pl.BlockSpec((MLA_HEAD_COLS, tq), lambda b, h, i: (h, b * nq + i)),
                  k_ctx, k_lat, vt_ctx, vt_lat,
                  pl.BlockSpec((tq, MLA_V), lambda b, h, i: (b * nq + i, z_block0 + h))],
        out_specs=pl.BlockSpec((tq, MLA_V), lambda b, h, i: (b * nq + i, h)),
        out_shape=jax.ShapeDtypeStruct((geom.batch * seq, MLA_WIDTH), MXU_DTYPE),
        compiler_params=sem, name="mla_attention",
    )(qt, k, k, vt, vt, proj)
    ctx = pl.pallas_call(
        functools.partial(_attn_kernel, n_segments=1, tk=ATTN_TK),
        grid=(geom.batch, MLA_HEADS, 1),
        in_specs=[pl.BlockSpec((MLA_HEAD_COLS, ctx_len), lambda b, h, i: (h, ctx_block0 + b)),
                  k_ctx, vt_ctx,
                  pl.BlockSpec((ctx_len, MLA_V), lambda b, h, i: (ctx_block0 + b, z_block0 + h))],
        out_specs=pl.BlockSpec((ctx_len, MLA_V), lambda b, h, i: (b, h)),
        out_shape=jax.ShapeDtypeStruct((geom.batch * ctx_len, MLA_WIDTH), MXU_DTYPE),
        compiler_params=sem, name="mla_attention_ctx",
    )(qt, k, vt, proj)
    return jnp.concatenate([lat, ctx], axis=0)


def _gdn_conv_kernel(cur_ref, prev_ref, next_ref, w_ref, o_ref, pad_ref, *, geom, tc):
    c = pl.program_id(1)
    pos, n_tiles = geom.seq_pos(pl.program_id(0))
    pad_ref[0:SUBLANES, :] = jnp.where(pos != 0, prev_ref[...], 0.0)
    pad_ref[SUBLANES:SUBLANES + ROW_TILE, :] = cur_ref[...]
    pad_ref[SUBLANES + ROW_TILE:, :] = jnp.where(pos != n_tiles - 1, next_ref[...], 0.0)
    w = w_ref[...]
    n_pad = pad_ref.shape[0]
    key_tiles = GDN_KEY // tc
    is_qk = c < 2 * key_tiles
    scale = jnp.where(c < key_tiles, GDN_DK ** -0.5, 1.0)
    for g in range(tc // GDN_DK):
        cols = slice(g * GDN_DK, (g + 1) * GDN_DK)
        padded = pad_ref[:, cols]
        y = jnp.zeros((ROW_TILE, GDN_DK), F32)
        for k in range(CONV_W):
            shift = (CONV_W // 2 - k) % n_pad
            taps = padded if shift == 0 else pltpu.roll(padded, shift, axis=0)
            y = y + taps[SUBLANES:SUBLANES + ROW_TILE] * w[k:k + 1, cols]
        y = _silu(y)
        inv = lax.rsqrt(jnp.sum(y * y, axis=-1, keepdims=True) + EPS) * scale
        o_ref[:, cols] = (y * jnp.where(is_qk, inv, 1.0)).astype(o_ref.dtype)


def _gdn_conv(proj, conv_w, geom):
    t = proj.shape[0]
    tc = 512
    halo_per_tile = ROW_TILE // SUBLANES
    last_halo = t // SUBLANES - 1
    return pl.pallas_call(
        functools.partial(_gdn_conv_kernel, geom=geom, tc=tc),
        grid=(t // ROW_TILE, GDN_CONV_CH // tc),
        in_specs=[pl.BlockSpec((ROW_TILE, tc), lambda i, c: (i, c)),
                  pl.BlockSpec((SUBLANES, tc), lambda i, c: (jnp.maximum(i * halo_per_tile - 1, 0), c)),
                  pl.BlockSpec((SUBLANES, tc), lambda i, c: (jnp.minimum((i + 1) * halo_per_tile, last_halo), c)),
                  pl.BlockSpec((CONV_W, tc), lambda i, c: (0, c))],
        out_specs=pl.BlockSpec((ROW_TILE, tc), lambda i, c: (i, c)),
        out_shape=jax.ShapeDtypeStruct((t, GDN_CONV_CH), MXU_DTYPE),
        scratch_shapes=[pltpu.VMEM((ROW_TILE + 2 * SUBLANES, tc), F32)],
        compiler_params=_params("arbitrary", "arbitrary"),
        name="gdn_conv",
    )(proj, proj, proj, conv_w)


def _split3(x):
    hi = x.astype(MXU_DTYPE)
    r = x - hi.astype(F32)
    mid = r.astype(MXU_DTYPE)
    lo = (r - mid.astype(F32)).astype(MXU_DTYPE)
    return hi, mid, lo


def _gdn_gates_kernel(ba_ref, alog_ref, dtb_ref, gb_ref, gbt_ref):
    half = LANES // 2
    ba = ba_ref[...]
    beta = _sigmoid(ba[:, :half])
    x = ba[:, half:] + dtb_ref[...]
    softplus = jnp.maximum(x, 0.0) + jnp.log(1.0 + jnp.exp(-jnp.abs(x)))
    g = -jnp.exp(alog_ref[...]) * softplus
    row = lax.broadcasted_iota(jnp.int32, (ROW_TILE, ROW_TILE), 0)
    col = lax.broadcasted_iota(jnp.int32, (ROW_TILE, ROW_TILE), 1)
    lower = (row >= col).astype(MXU_DTYPE)
    upper = (row <= col).astype(MXU_DTYPE)
    pre = jnp.zeros((ROW_TILE, half), F32)
    suf = jnp.zeros((ROW_TILE, half), F32)
    for piece in _split3(g):
        pre = pre + jnp.dot(lower, piece, preferred_element_type=F32)
        suf = suf + jnp.dot(upper, piece, preferred_element_type=F32)
    lane = lax.broadcasted_iota(jnp.int32, (ROW_TILE, half), 1)
    gcum = jnp.where(lane < half // 2, pre, suf)
    gb = jnp.concatenate([beta, gcum], axis=-1)
    gb_ref[...] = gb
    gbt_ref[...] = gb.T


def _gdn_gates(ba, a_log, dt_bias):
    t = ba.shape[0]
    half = LANES // 2
    return pl.pallas_call(
        _gdn_gates_kernel,
        grid=(t // ROW_TILE,),
        in_specs=[pl.BlockSpec((ROW_TILE, LANES), lambda i: (i, 0)),
                  pl.BlockSpec((1, half), lambda i: (0, 0)),
                  pl.BlockSpec((1, half), lambda i: (0, 0))],
        out_specs=[pl.BlockSpec((ROW_TILE, LANES), lambda i: (i, 0)),
                   pl.BlockSpec((LANES, ROW_TILE), lambda i: (0, i))],
        out_shape=[jax.ShapeDtypeStruct((t, LANES), F32), jax.ShapeDtypeStruct((LANES, t), F32)],
        compiler_params=_params("arbitrary"),
        name="gdn_gates",
    )(ba, a_log.reshape(1, half), dt_bias.reshape(1, half))


def _mxu_dot(a, b):
    return jnp.dot(a.astype(MXU_DTYPE), b.astype(MXU_DTYPE), preferred_element_type=F32)


def _emit_round_robin(programs):
    live = list(programs)
    while live:
        for prog in list(live):
            try:
                next(prog)
            except StopIteration:
                live.remove(prog)


def _gdn_scan_kernel(*refs, backward):
    if backward:
        q_ref, k_ref, v_ref, gb_ref, gbt_ref, fwd_ref, z_ref, nw_ref, o_ref, state_ref = refs
    else:
        q_ref, k_ref, v_ref, gb_ref, gbt_ref, o_ref, state_ref = refs
    n = ROW_TILE
    rep = GDN_V_HEADS // GDN_K_HEADS
    d = int(backward)

    @pl.when(pl.program_id(2) == 0)
    def _():
        state_ref[...] = jnp.zeros(state_ref.shape, F32)

    row = lax.broadcasted_iota(jnp.int32, (n, n), 0)
    col = lax.broadcasted_iota(jnp.int32, (n, n), 1)
    ahead = col - row if backward else row - col
    incl = ahead >= 0
    strict = ahead > 0
    block_xor = row ^ col
    last = 0 if backward else n - 1
    lane = lax.broadcasted_iota(jnp.int32, (n, LANES), 1)
    sub1 = lax.broadcasted_iota(jnp.int32, (n, 1), 0)
    gb = gb_ref[...]
    half = LANES // 2

    c = GDN_SOLVE_BLOCK
    log_c = int(np.log2(c))
    p_row = lax.broadcasted_iota(jnp.int32, (c, n), 0)
    p_lane = lax.broadcasted_iota(jnp.int32, (c, n), 1)
    p_blk = p_lane >> log_c
    p_xor = p_row ^ (p_lane & (c - 1))

    def pack(mat):
        out = mat[n - c:]
        for a in range(n // c - 2, -1, -1):
            out = jnp.where(p_blk == a, mat[a * c:(a + 1) * c], out)
        return out

    def unpack(packed):
        return jnp.concatenate([jnp.where(p_blk == a, packed, 0.0) for a in range(n // c)], axis=0)

    def key_head_program(kh):
        q = q_ref[:, kh * GDN_DK:(kh + 1) * GDN_DK]
        k = k_ref[:, kh * GDN_DK:(kh + 1) * GDN_DK]
        gram = _nt_dot(jnp.concatenate([q, k], axis=0), k)
        qk, kk = gram[:n], gram[n:]
        probs = []
        for e in range(rep):
            idx = kh * rep + e
            c_beta = d * GDN_V_HEADS + (pl.program_id(1) * GDN_SCAN_K_HEADS + kh) * rep + e
            c_g = half + c_beta
            beta_c = jnp.sum(jnp.where(lane == c_beta, gb, 0.0), axis=-1, keepdims=True)
            g_c = jnp.sum(jnp.where(lane == c_g, gb, 0.0), axis=-1, keepdims=True)
            beta_r = gbt_ref[pl.ds(c_beta, 1), :]
            g_r = gbt_ref[pl.ds(c_g, 1), :]
            g_last = jnp.sum(jnp.where(sub1 == last, g_c, 0.0), axis=0, keepdims=True)
            decay = jnp.exp(jnp.where(incl, g_c - g_r, -jnp.inf))
            probs.append(dict(
                idx=idx, v=v_ref[:, idx * GDN_DV:(idx + 1) * GDN_DV], g_c=g_c, g_last=g_last,
                beta_r=beta_r, qk_dec=qk * decay, a_mat=jnp.where(strict, beta_c * kk * decay, 0.0)))
        yield
        packed_a = [pack(p["a_mat"]) for p in probs]
        packed_t = [(p_xor == 0).astype(F32) - jnp.where(p_xor == 1, pa, 0.0) for pa in packed_a]
        for log_m in range(1, log_c):
            off = (p_xor >> log_m) == 1
            xs = [_mxu_dot(jnp.where(off, pa, 0.0), unpack(pt)) for pa, pt in zip(packed_a, packed_t)]
            yield
            packed_t = [pt - _mxu_dot(pt, unpack(x)) for pt, x in zip(packed_t, xs)]
            yield
        invs = [unpack(pt) for pt in packed_t]
        for log_m in range(log_c, int(np.log2(n))):
            off = (block_xor >> log_m) == 1
            xs = [_mxu_dot(jnp.where(off, p["a_mat"], 0.0), inv) for p, inv in zip(probs, invs)]
            yield
            invs = [inv - _mxu_dot(inv, x) for inv, x in zip(invs, xs)]
            yield
        egs = [jnp.exp(p["g_c"]) for p in probs]
        uws = [_mxu_dot(inv * p["beta_r"], jnp.concatenate([p["v"].astype(F32), k.astype(F32) * eg], axis=1))
               for p, inv, eg in zip(probs, invs, egs)]
        yield
        states = [state_ref[p["idx"]] for p in probs]
        from_state = [_mxu_dot(jnp.concatenate([uw[:, GDN_DV:], q.astype(F32) * eg], axis=0), s)
                      for uw, eg, s in zip(uws, egs, states)]
        yield
        v_news = [uw[:, :GDN_DV] - r[:n] for uw, r in zip(uws, from_state)]
        outs = [r[n:] + _mxu_dot(p["qk_dec"], v_new) for p, r, v_new in zip(probs, from_state, v_news)]
        yield
        for p, s, v_new, o in zip(probs, states, v_news, outs):
            i = p["idx"]
            cols = slice(i * GDN_DV, (i + 1) * GDN_DV)
            if backward:
                o = o + fwd_ref[:, cols]
                y = o * lax.rsqrt(jnp.mean(o * o, axis=-1, keepdims=True) + EPS) * nw_ref[...]
                o_ref[:, cols] = (y * _silu(z_ref[:, cols])).astype(o_ref.dtype)
            else:
                o_ref[:, cols] = o
            k_dec = k.astype(F32) * jnp.exp(p["g_last"] - p["g_c"])
            state_ref[i] = s * jnp.exp(p["g_last"]) + lax.dot_general(
                k_dec.astype(MXU_DTYPE), v_new.astype(MXU_DTYPE), (((0,), (0,)), ((), ())),
                preferred_element_type=F32)

    _emit_round_robin([key_head_program(kh) for kh in range(GDN_SCAN_K_HEADS)])


def _gdn_scan(qkv, gb, gbt, proj, o_norm, geom):
    t = qkv.shape[0]
    rep = GDN_V_HEADS // GDN_K_HEADS
    kw = GDN_SCAN_K_HEADS * GDN_DK
    vw = GDN_SCAN_K_HEADS * rep * GDN_DV
    grid = (geom.batch, GDN_K_HEADS // GDN_SCAN_K_HEADS, geom.n_ctx + geom.n_lat)
    scratch = [pltpu.VMEM((GDN_SCAN_K_HEADS * rep, GDN_DK, GDN_DV), F32)]
    sem = _params("arbitrary", "arbitrary", "arbitrary")

    def specs(backward):
        def tile(b, i):
            ctx_pos = geom.n_ctx - 1 - i if backward else i
            lat_pos = geom.n_lat - 1 - (i - geom.n_ctx) if backward else i - geom.n_ctx
            return jnp.where(i < geom.n_ctx, geom.lat_tiles + b * geom.n_ctx + ctx_pos, b * geom.n_lat + lat_pos)

        ins = [pl.BlockSpec((ROW_TILE, kw), lambda b, h, i: (tile(b, i), h)),
               pl.BlockSpec((ROW_TILE, kw), lambda b, h, i: (tile(b, i), GDN_KEY // kw + h)),
               pl.BlockSpec((ROW_TILE, vw), lambda b, h, i: (tile(b, i), 2 * GDN_KEY // vw + h)),
               pl.BlockSpec((ROW_TILE, LANES), lambda b, h, i: (tile(b, i), 0)),
               pl.BlockSpec((LANES, ROW_TILE), lambda b, h, i: (0, tile(b, i)))]
        out = pl.BlockSpec((ROW_TILE, vw), lambda b, h, i: (tile(b, i), h))
        z = pl.BlockSpec((ROW_TILE, vw), lambda b, h, i: (tile(b, i), GDN_CONV_CH // vw + h))
        return ins, out, z

    ins, out, _ = specs(False)
    o_fwd = pl.pallas_call(
        functools.partial(_gdn_scan_kernel, backward=False), grid=grid, in_specs=ins, out_specs=out,
        out_shape=jax.ShapeDtypeStruct((t, GDN_VAL), F32), scratch_shapes=scratch, compiler_params=sem,
        name="gdn_scan_fwd",
    )(qkv, qkv, qkv, gb, gbt)
    ins, out, z = specs(True)
    return pl.pallas_call(
        functools.partial(_gdn_scan_kernel, backward=True), grid=grid,
        in_specs=ins + [out, z, pl.BlockSpec((1, GDN_DV), lambda b, h, i: (0, 0))], out_specs=out,
        out_shape=jax.ShapeDtypeStruct((t, GDN_VAL), MXU_DTYPE), scratch_shapes=scratch, compiler_params=sem,
        name="gdn_scan_bwd",
    )(qkv, qkv, qkv, gb, gbt, o_fwd, proj, o_norm.reshape(1, GDN_DV))


def _final_norm_kernel(x_ref, w_ref, o_ref):
    x = x_ref[...]
    o_ref[...] = x * lax.rsqrt(jnp.mean(x * x, axis=-1, keepdims=True) + EPS) * w_ref[...]


def _final_norm(h, w, geom):
    d = h.shape[1]
    return pl.pallas_call(
        _final_norm_kernel,
        grid=(geom.lat_tiles,),
        in_specs=[pl.BlockSpec((ROW_TILE, d), lambda i: (i, 0)),
                  pl.BlockSpec((1, d), lambda i: (0, 0))],
        out_specs=pl.BlockSpec((ROW_TILE, d), lambda i: (i, 0)),
        out_shape=jax.ShapeDtypeStruct((geom.lat_tiles * ROW_TILE, d), F32),
        compiler_params=_params("arbitrary"),
        name="final_norm",
    )(h, w.reshape(1, d))


def _rope_table(batch, ctx_len, seq):
    pos = jnp.arange(seq)
    r = (pos // GRID_W).astype(F32)
    col = (pos % GRID_W).astype(F32)
    half = MLA_ROPE // 2
    inv = ROPE_THETA ** (-jnp.arange(0, half, 2, dtype=F32) / half)
    ang = jnp.concatenate([r[:, None] * inv, col[:, None] * inv], axis=-1)
    cos, sin = jnp.cos(ang), jnp.sin(ang)
    lat = jnp.concatenate([cos, cos, -sin, sin], axis=-1)
    ident = jnp.concatenate([jnp.ones((ctx_len, MLA_ROPE), F32), jnp.zeros((ctx_len, MLA_ROPE), F32)], axis=-1)
    return jnp.concatenate([jnp.tile(lat, (batch, 1)), jnp.tile(ident, (batch, 1))], axis=0)


_ROPE_SWAP = np.concatenate([np.arange(MLA_ROPE // 2, MLA_ROPE), np.arange(MLA_ROPE // 2)])


def _mla_weights(w_in, w_uq, w_ukv):
    lo = Q_LORA + KV_LORA
    k_rope = w_in[:, lo:lo + MLA_ROPE]
    w_in_p = jnp.concatenate([w_in[:, :lo], w_in[:, lo + MLA_ROPE:], k_rope, k_rope[:, _ROPE_SWAP]], axis=1)
    head = np.concatenate([np.arange(MLA_QK), MLA_NOPE + _ROPE_SWAP])
    cols = (np.arange(MLA_HEADS)[:, None] * MLA_QK + head[None, :]).reshape(-1)
    wqt = w_uq[:, cols].T.astype(MXU_DTYPE)
    w_kv = w_ukv.reshape(KV_LORA, MLA_HEADS, MLA_NOPE + MLA_V)
    wk = w_kv[:, :, :MLA_NOPE].reshape(KV_LORA, MLA_HEADS * MLA_NOPE).astype(MXU_DTYPE)
    wvt = w_kv[:, :, MLA_NOPE:].reshape(KV_LORA, MLA_WIDTH).T.astype(MXU_DTYPE)
    return w_in_p, wqt, wk, wvt


def _mla_layer(h, a, mod, tab, tabt, w_in, q_norm, w_uq, kv_norm, w_ukv, w_o_all, layer, geom):
    w_in_p, wqt, wk, wvt = _mla_weights(w_in, w_uq, w_ukv)
    proj = _matmul(a, w_in_p[None], 0, tn=640, name="mla_in")
    qt, k, vt = _mla_up(proj, q_norm, kv_norm, wqt, wk, wvt, tab, tabt)
    gated = _attention(qt, k, vt, proj, geom)
    return _matmul_residual(gated, w_o_all, layer, h, mod, geom, name="mla_out")


def _gdn_layer(h, a, mod, w_in_all, conv_w, a_log, dt_bias, o_norm, w_o_all, layer, geom):
    proj = _matmul(a, w_in_all, layer, tn=1024, n_blocks=GDN_QKVZ // 1024, name="gdn_in")
    ba = _matmul(a, w_in_all, layer, tn=LANES, first_block=GDN_QKVZ // LANES, n_blocks=1, name="gdn_in_gates")
    qkv = _gdn_conv(proj, conv_w, geom)
    gb, gbt = _gdn_gates(ba, a_log, dt_bias)
    gated = _gdn_scan(qkv, gb, gbt, proj, o_norm, geom)
    return _matmul_residual(gated, w_o_all, layer, h, mod, geom, name="gdn_out")


def kernel(x, c, ctx, c_ctx, ada_w, ada_b, norm_w, mla_w_in, mla_q_norm, mla_w_uq, mla_kv_norm, mla_w_ukv, mla_w_o, gdn_w_in, gdn_conv_w, gdn_a_log, gdn_dt_bias, gdn_o_norm, gdn_w_o, final_norm):
    batch, seq, d = x.shape
    ctx_len = ctx.shape[1]
    n_rows = batch * (seq + ctx_len)
    assert d == D_MODEL and seq % ATTN_TQ == 0 and ctx_len % ROW_TILE == 0 and seq % GRID_W == 0
    assert batch < SUBLANES and n_rows % MM_TM == 0 and (batch * seq) % ctx_len == 0
    geom = Geom(batch=batch, n_lat=seq // ROW_TILE, n_ctx=ctx_len // ROW_TILE)

    h = jnp.concatenate([x.reshape(batch * seq, d), ctx.reshape(batch * ctx_len, d)], axis=0)
    conds = jnp.zeros((SUBLANES, d), F32).at[:batch].set(c).at[batch].set(c_ctx)
    mods = _adaln(conds, ada_w, ada_b)
    tab = _rope_table(batch, ctx_len, seq)
    tabt = tab.T

    for i in range(DEPTH):
        a = _prenorm(h, norm_w[i], mods[i], geom)
        j = i // 2
        if i % 2 == 0:
            h = _mla_layer(h, a, mods[i], tab, tabt, mla_w_in[j], mla_q_norm[j], mla_w_uq[j], mla_kv_norm[j],
                           mla_w_ukv[j], mla_w_o, j, geom)
        else:
            h = _gdn_layer(h, a, mods[i], gdn_w_in, gdn_conv_w[j], gdn_a_log[j], gdn_dt_bias[j],
                           gdn_o_norm[j], gdn_w_o, j, geom)
    return _final_norm(h, final_norm, geom).reshape(batch, seq, d)
```

```python
import functools
import math
from typing import NamedTuple

import jax
import jax.numpy as jnp
import numpy as np
from jax import lax
from jax.experimental import pallas as pl
from jax.experimental.pallas import tpu as pltpu

F32 = jnp.float32
MXU_DTYPE = jnp.bfloat16

D_MODEL = 2048
DEPTH = 4
EPS = 1e-6
GRID_W = 64
ROPE_THETA = 10000.0
MLA_HEADS = 16
MLA_NOPE = 128
MLA_ROPE = 64
MLA_V = 128
MLA_QK = MLA_NOPE + MLA_ROPE
Q_LORA = 512
KV_LORA = 512
MLA_WIDTH = MLA_HEADS * MLA_V
MLA_HEAD_COLS = 256
MLA_VT_ROWS = MLA_V + 16
MLA_IN_PAD = Q_LORA + KV_LORA + MLA_WIDTH + 2 * MLA_ROPE
GDN_K_HEADS = 16
GDN_V_HEADS = 32
GDN_DK = 128
GDN_DV = 128
GDN_KEY = GDN_K_HEADS * GDN_DK
GDN_VAL = GDN_V_HEADS * GDN_DV
GDN_CONV_CH = 2 * GDN_KEY + GDN_VAL
GDN_QKVZ = GDN_CONV_CH + GDN_VAL
CONV_W = 5

LANES = 128
SUBLANES = 8
ROW_TILE = 256
MM_TM = 512
MM_TM_MAX = 1100
GDN_CONV_TC = 2048
ATTN_TQ = 1024
ATTN_TK = 256
GDN_SCAN_K_HEADS = 4
GDN_SOLVE_BLOCK = 64
VMEM_LIMIT_BYTES = 50 * 1024 * 1024


class Geom(NamedTuple):
    batch: int
    n_lat: int
    n_ctx: int

    @property
    def lat_tiles(self):
        return self.batch * self.n_lat

    def mod_row(self, tile):
        return jnp.where(tile < self.lat_tiles, tile // self.n_lat, self.batch)

    def seq_pos(self, tile):
        is_lat = tile < self.lat_tiles
        pos = jnp.where(is_lat, tile % self.n_lat, (tile - self.lat_tiles) % self.n_ctx)
        return pos, jnp.where(is_lat, self.n_lat, self.n_ctx)


def _params(*sem):
    return pltpu.CompilerParams(dimension_semantics=sem, vmem_limit_bytes=VMEM_LIMIT_BYTES)


def _sigmoid(x):
    return 1.0 / (1.0 + jnp.exp(-x))


def _silu(x):
    return x * _sigmoid(x)


def _nt_dot(a, b):
    return lax.dot_general(a, b, (((1,), (1,)), ((), ())), preferred_element_type=F32)


def _adaln_kernel(c_ref, w_ref, b_ref, o_ref):
    a = _silu(c_ref[...]).astype(MXU_DTYPE)
    w = w_ref[0].astype(MXU_DTYPE)
    o_ref[0] = jnp.dot(a, w, preferred_element_type=F32) + b_ref[0]


def _adaln(conds, ada_w, ada_b):
    depth, d, n = ada_w.shape
    tn = 768
    return pl.pallas_call(
        _adaln_kernel,
        grid=(depth, n // tn),
        in_specs=[pl.BlockSpec((SUBLANES, d), lambda l, j: (0, 0)),
                  pl.BlockSpec((1, d, tn), lambda l, j: (l, 0, j)),
                  pl.BlockSpec((1, 1, tn), lambda l, j: (l, 0, j))],
        out_specs=pl.BlockSpec((1, SUBLANES, tn), lambda l, j: (l, 0, j)),
        out_shape=jax.ShapeDtypeStruct((depth, SUBLANES, n), F32),
        compiler_params=_params("arbitrary", "arbitrary"),
        name="adaln",
    )(conds, ada_w, ada_b.reshape(depth, 1, n))


def _prenorm_kernel(x_ref, nw_ref, mod_ref, o_ref, *, geom):
    m = mod_ref[pl.ds(geom.mod_row(pl.program_id(0)), 1), :]
    x = x_ref[...]
    y = x * lax.rsqrt(jnp.mean(x * x, axis=-1, keepdims=True) + EPS) * nw_ref[...]
    o_ref[...] = (y * (1.0 + m[:, D_MODEL:2 * D_MODEL]) + m[:, :D_MODEL]).astype(o_ref.dtype)


def _prenorm(h, norm_w, mod, geom):
    t, d = h.shape
    return pl.pallas_call(
        functools.partial(_prenorm_kernel, geom=geom),
        grid=(t // ROW_TILE,),
        in_specs=[pl.BlockSpec((ROW_TILE, d), lambda i: (i, 0)),
                  pl.BlockSpec((1, d), lambda i: (0, 0)),
                  pl.BlockSpec((SUBLANES, 3 * d), lambda i: (0, 0))],
        out_specs=pl.BlockSpec((ROW_TILE, d), lambda i: (i, 0)),
        out_shape=jax.ShapeDtypeStruct((t, d), MXU_DTYPE),
        compiler_params=_params("arbitrary"),
        name="prenorm",
    )(h, norm_w.reshape(1, d), mod)


def _mm_kernel(a_ref, w_ref, o_ref, w_mxu_ref):
    @pl.when(pl.program_id(1) == 0)
    def _():
        w_mxu_ref[...] = w_ref[0].astype(MXU_DTYPE)

    o_ref[...] = jnp.dot(a_ref[...], w_mxu_ref[...], preferred_element_type=F32).astype(o_ref.dtype)


def _matmul(a, w, layer, *, tn, first_block=0, n_blocks=None, out_dtype=F32, name="matmul"):
    m, k = a.shape
    n_blocks = w.shape[2] // tn if n_blocks is None else n_blocks
    tm = max(t for t in range(16, MM_TM_MAX + 1, 16) if m % t == 0)
    return pl.pallas_call(
        _mm_kernel,
        grid=(n_blocks, m // tm),
        in_specs=[pl.BlockSpec((tm, k), lambda j, i: (i, 0)),
                  pl.BlockSpec((1, k, tn), lambda j, i: (layer, 0, first_block + j))],
        out_specs=pl.BlockSpec((tm, tn), lambda j, i: (i, j)),
        out_shape=jax.ShapeDtypeStruct((m, n_blocks * tn), out_dtype),
        scratch_shapes=[pltpu.VMEM((k, tn), MXU_DTYPE)],
        compiler_params=_params("arbitrary", "arbitrary"),
        name=name,
    )(a, w)


def _mm_res_kernel(a_ref, w_ref, res_ref, gate_ref, o_ref, w_mxu_ref, *, geom):
    @pl.when(pl.program_id(1) == 0)
    def _():
        w_mxu_ref[...] = w_ref[0].astype(MXU_DTYPE)

    acc = jnp.dot(a_ref[...], w_mxu_ref[...], preferred_element_type=F32)
    for s in range(MM_TM // ROW_TILE):
        tile = pl.program_id(1) * (MM_TM // ROW_TILE) + s
        g = gate_ref[pl.ds(geom.mod_row(tile), 1), :]
        rows = slice(s * ROW_TILE, (s + 1) * ROW_TILE)
        o_ref[rows, :] = res_ref[rows, :] + g * acc[rows, :]


def _matmul_residual(a, w, layer, res, mod, geom, *, tn=512, name="matmul_res"):
    m, k = a.shape
    n = w.shape[2]
    gate_block0 = 2 * D_MODEL // tn
    return pl.pallas_call(
        functools.partial(_mm_res_kernel, geom=geom),
        grid=(n // tn, m // MM_TM),
        in_specs=[pl.BlockSpec((MM_TM, k), lambda j, i: (i, 0)),
                  pl.BlockSpec((1, k, tn), lambda j, i: (layer, 0, j)),
                  pl.BlockSpec((MM_TM, tn), lambda j, i: (i, j)),
                  pl.BlockSpec((SUBLANES, tn), lambda j, i: (0, gate_block0 + j))],
        out_specs=pl.BlockSpec((MM_TM, tn), lambda j, i: (i, j)),
        out_shape=jax.ShapeDtypeStruct((m, n), F32),
        scratch_shapes=[pltpu.VMEM((k, tn), MXU_DTYPE)],
        compiler_params=_params("arbitrary", "arbitrary"),
        name=name,
    )(a, w, res, mod)


def _mla_up_kernel(cq_ref, ckv_ref, kr_ref, qn_ref, kvn_ref, wqt_ref, wk_ref, wvt_ref, tab_ref, tabt_ref,
                   qt_ref, k_ref, vt_ref):
    def rms(x, w):
        return (x * lax.rsqrt(jnp.mean(x * x, axis=-1, keepdims=True) + EPS) * w).astype(MXU_DTYPE)

    cq = rms(cq_ref[...], qn_ref[...])
    ckv = rms(ckv_ref[...], kvn_ref[...])
    tm = cq.shape[0]
    t = kr_ref[...] * tab_ref[...]
    lane = lax.broadcasted_iota(jnp.int32, t.shape, 1)
    k_rot = jnp.where(lane < MLA_ROPE, t + pltpu.roll(t, MLA_ROPE, axis=1), 0.0).astype(k_ref.dtype)
    tabt = tabt_ref[...]
    zeros = jnp.zeros((MLA_ROPE, tm), qt_ref.dtype)
    q_scale = MLA_QK ** -0.5 * math.log2(math.e)
    for h in range(MLA_HEADS):
        q0 = h * MLA_HEAD_COLS
        qh = _nt_dot(wqt_ref[q0:q0 + MLA_HEAD_COLS, :], cq) * q_scale
        qt_ref[q0:q0 + MLA_NOPE, :] = qh[:MLA_NOPE].astype(qt_ref.dtype)
        tq = qh[MLA_NOPE:] * tabt
        qt_ref[q0 + MLA_NOPE:q0 + MLA_NOPE + MLA_ROPE, :] = (tq[:MLA_ROPE] + tq[MLA_ROPE:]).astype(qt_ref.dtype)
        qt_ref[q0 + MLA_NOPE + MLA_ROPE:q0 + MLA_HEAD_COLS, :] = zeros
        k_ref[:, q0:q0 + MLA_NOPE] = jnp.dot(ckv, wk_ref[:, h * MLA_NOPE:(h + 1) * MLA_NOPE],
                                            preferred_element_type=F32).astype(k_ref.dtype)
        k_ref[:, q0 + MLA_NOPE:q0 + MLA_HEAD_COLS] = k_rot
        v0 = h * MLA_VT_ROWS
        vt_ref[v0:v0 + MLA_V, :] = _nt_dot(wvt_ref[h * MLA_V:(h + 1) * MLA_V, :], ckv).astype(vt_ref.dtype)
        vt_ref[v0 + MLA_V:v0 + MLA_VT_ROWS, :] = jnp.ones((MLA_VT_ROWS - MLA_V, tm), vt_ref.dtype)


def _mla_up(proj, q_norm, kv_norm, wqt, wk, wvt, tab, tabt):
    t = proj.shape[0]
    tm = ROW_TILE
    nq = MLA_HEADS * MLA_HEAD_COLS
    kr_block = (Q_LORA + KV_LORA + MLA_WIDTH) // LANES
    const = lambda i: (0, 0)
    return pl.pallas_call(
        _mla_up_kernel,
        grid=(t // tm,),
        in_specs=[pl.BlockSpec((tm, Q_LORA), lambda i: (i, 0)),
                  pl.BlockSpec((tm, KV_LORA), lambda i: (i, 1)),
                  pl.BlockSpec((tm, LANES), lambda i: (i, kr_block)),
                  pl.BlockSpec((1, Q_LORA), const),
                  pl.BlockSpec((1, KV_LORA), const),
                  pl.BlockSpec(wqt.shape, const),
                  pl.BlockSpec(wk.shape, const),
                  pl.BlockSpec(wvt.shape, const),
                  pl.BlockSpec((tm, LANES), lambda i: (i, 0)),
                  pl.BlockSpec((LANES, tm), lambda i: (0, i))],
        out_specs=[pl.BlockSpec((nq, tm), lambda i: (0, i)),
                   pl.BlockSpec((tm, nq), lambda i: (i, 0)),
                   pl.BlockSpec((MLA_HEADS * MLA_VT_ROWS, tm), lambda i: (0, i))],
        out_shape=[jax.ShapeDtypeStruct((nq, t), MXU_DTYPE),
                   jax.ShapeDtypeStruct((t, nq), MXU_DTYPE),
                   jax.ShapeDtypeStruct((MLA_HEADS * MLA_VT_ROWS, t), MXU_DTYPE)],
        compiler_params=_params("arbitrary"),
        name="mla_up",
    )(proj, proj, proj, q_norm.reshape(1, -1), kv_norm.reshape(1, -1), wqt, wk, wvt, tab, tabt)


def _attn_kernel(*refs, n_segments, tk):
    qt_ref = refs[0]
    k_refs = refs[1:1 + n_segments]
    vt_refs = refs[1 + n_segments:1 + 2 * n_segments]
    z_ref, o_ref = refs[1 + 2 * n_segments:]
    qt = qt_ref[...]
    tq = qt.shape[1]
    m = jnp.full((1, tq), -jnp.inf, F32)
    acc = jnp.zeros((MLA_VT_ROWS, tq), F32)
    tiles = [(k_ref, vt_ref, j * tk) for k_ref, vt_ref in zip(k_refs, vt_refs)
             for j in range(k_ref.shape[0] // tk)]

    def scores(tile):
        k_ref, _, off = tile
        return jnp.dot(k_ref[off:off + tk, :], qt, preferred_element_type=F32)

    s_next = scores(tiles[0])
    for n, (_, vt_ref, off) in enumerate(tiles):
        s = s_next
        if n + 1 < len(tiles):
            s_next = scores(tiles[n + 1])
        m_new = jnp.maximum(m, jnp.max(s, axis=0, keepdims=True))
        alpha = jnp.exp2(m - m_new)
        p = jnp.exp2(s - m_new)
        acc = alpha * acc + jnp.dot(vt_ref[:, off:off + tk], p.astype(MXU_DTYPE), preferred_element_type=F32)
        m = m_new
    o = (acc[:MLA_V] / acc[MLA_V:MLA_V + 1]).T
    o_ref[...] = (o * _silu(z_ref[...])).astype(o_ref.dtype)


def _attention(qt, k, vt, proj, geom):
    seq, ctx_len = geom.n_lat * ROW_TILE, geom.n_ctx * ROW_TILE
    ctx_block0 = geom.batch * seq // ctx_len
    z_block0 = (Q_LORA + KV_LORA) // LANES
    k_lat = pl.BlockSpec((seq, MLA_HEAD_COLS), lambda b, h, i: (b, h))
    k_ctx = pl.BlockSpec((ctx_len, MLA_HEAD_COLS), lambda b, h, i: (ctx_block0 + b, h))
    vt_lat = pl.BlockSpec((MLA_VT_ROWS, seq), lambda b, h, i: (h, b))
    vt_ctx = pl.BlockSpec((MLA_VT_ROWS, ctx_len), lambda b, h, i: (h, ctx_block0 + b))
    sem = _params("arbitrary", "arbitrary", "arbitrary")

    tq = ATTN_TQ
    nq = seq // tq
    lat = pl.pallas_call(
        functools.partial(_attn_kernel, n_segments=2, tk=ATTN_TK),
        grid=(geom.batch, MLA_HEADS, nq),
        in_specs=[pl.BlockSpec((MLA_HEAD_COLS, tq), lambda b, h, i: (h, b * nq + i)),
                  k_ctx, k_lat, vt_ctx, vt_lat,
                  pl.BlockSpec((tq, MLA_V), lambda b, h, i: (b * nq + i, z_block0 + h))],
        out_specs=pl.BlockSpec((tq, MLA_V), lambda b, h, i: (b * nq + i, h)),
        out_shape=jax.ShapeDtypeStruct((geom.batch * seq, MLA_WIDTH), MXU_DTYPE),
        compiler_params=sem, name="mla_attention",
    )(qt, k, k, vt, vt, proj)
    ctx = pl.pallas_call(
        functools.partial(_attn_kernel, n_segments=1, tk=ATTN_TK),
        grid=(geom.batch, MLA_HEADS, 1),
        in_specs=[pl.BlockSpec((MLA_HEAD_COLS, ctx_len), lambda b, h, i: (h, ctx_block0 + b)),
                  k_ctx, vt_ctx,
                  pl.BlockSpec((ctx_len, MLA_V), lambda b, h, i: (ctx_block0 + b, z_block0 + h))],
        out_specs=pl.BlockSpec((ctx_len, MLA_V), lambda b, h, i: (b, h)),
        out_shape=jax.ShapeDtypeStruct((geom.batch * ctx_len, MLA_WIDTH), MXU_DTYPE),
        compiler_params=sem, name="mla_attention_ctx",
    )(qt, k, vt, proj)
    return jnp.concatenate([lat, ctx], axis=0)


def _gdn_conv_kernel(cur_ref, prev_ref, next_ref, w_ref, o_ref, pad_ref, *, geom, tc):
    c = pl.program_id(1)
    pos, n_tiles = geom.seq_pos(pl.program_id(0))
    pad_ref[0:SUBLANES, :] = jnp.where(pos != 0, prev_ref[...], 0.0)
    pad_ref[SUBLANES:SUBLANES + ROW_TILE, :] = cur_ref[...]
    pad_ref[SUBLANES + ROW_TILE:, :] = jnp.where(pos != n_tiles - 1, next_ref[...], 0.0)
    w = w_ref[...]
    n_pad = pad_ref.shape[0]
    key_tiles = GDN_KEY // tc
    is_qk = c < 2 * key_tiles
    scale = jnp.where(c < key_tiles, GDN_DK ** -0.5, 1.0)
    for g in range(tc // GDN_DK):
        cols = slice(g * GDN_DK, (g + 1) * GDN_DK)
        padded = pad_ref[:, cols]
        y = jnp.zeros((ROW_TILE, GDN_DK), F32)
        for k in range(CONV_W):
            shift = (CONV_W // 2 - k) % n_pad
            taps = padded if shift == 0 else pltpu.roll(padded, shift, axis=0)
            y = y + taps[SUBLANES:SUBLANES + ROW_TILE] * w[k:k + 1, cols]
        y = _silu(y)
        inv = lax.rsqrt(jnp.sum(y * y, axis=-1, keepdims=True) + EPS) * scale
        o_ref[:, cols] = (y * jnp.where(is_qk, inv, 1.0)).astype(o_ref.dtype)


def _gdn_conv(proj, conv_w, geom):
    t = proj.shape[0]
    tc = GDN_CONV_TC
    halo_per_tile = ROW_TILE // SUBLANES
    last_halo = t // SUBLANES - 1
    return pl.pallas_call(
        functools.partial(_gdn_conv_kernel, geom=geom, tc=tc),
        grid=(t // ROW_TILE, GDN_CONV_CH // tc),
        in_specs=[pl.BlockSpec((ROW_TILE, tc), lambda i, c: (i, c)),
                  pl.BlockSpec((SUBLANES, tc), lambda i, c: (jnp.maximum(i * halo_per_tile - 1, 0), c)),
                  pl.BlockSpec((SUBLANES, tc), lambda i, c: (jnp.minimum((i + 1) * halo_per_tile, last_halo), c)),
                  pl.BlockSpec((CONV_W, tc), lambda i, c: (0, c))],
        out_specs=pl.BlockSpec((ROW_TILE, tc), lambda i, c: (i, c)),
        out_shape=jax.ShapeDtypeStruct((t, GDN_CONV_CH), MXU_DTYPE),
        scratch_shapes=[pltpu.VMEM((ROW_TILE + 2 * SUBLANES, tc), F32)],
        compiler_params=_params("arbitrary", "arbitrary"),
        name="gdn_conv",
    )(proj, proj, proj, conv_w)


def _split3(x):
    hi = x.astype(MXU_DTYPE)
    r = x - hi.astype(F32)
    mid = r.astype(MXU_DTYPE)
    lo = (r - mid.astype(F32)).astype(MXU_DTYPE)
    return hi, mid, lo


def _gdn_gates_kernel(ba_ref, alog_ref, dtb_ref, gb_ref, gbt_ref):
    half = LANES // 2
    ba = ba_ref[...]
    beta = _sigmoid(ba[:, :half])
    x = ba[:, half:] + dtb_ref[...]
    softplus = jnp.maximum(x, 0.0) + jnp.log(1.0 + jnp.exp(-jnp.abs(x)))
    g = -jnp.exp(alog_ref[...]) * softplus * math.log2(math.e)
    row = lax.broadcasted_iota(jnp.int32, (ROW_TILE, ROW_TILE), 0)
    col = lax.broadcasted_iota(jnp.int32, (ROW_TILE, ROW_TILE), 1)
    lower = (row >= col).astype(MXU_DTYPE)
    upper = (row <= col).astype(MXU_DTYPE)
    pre = jnp.zeros((ROW_TILE, half), F32)
    suf = jnp.zeros((ROW_TILE, half), F32)
    for piece in _split3(g):
        pre = pre + jnp.dot(lower, piece, preferred_element_type=F32)
        suf = suf + jnp.dot(upper, piece, preferred_element_type=F32)
    lane = lax.broadcasted_iota(jnp.int32, (ROW_TILE, half), 1)
    gcum = jnp.where(lane < half // 2, pre, suf)
    gb = jnp.concatenate([beta, gcum], axis=-1)
    gb_ref[...] = gb
    gbt_ref[...] = gb.T


def _gdn_gates(ba, a_log, dt_bias):
    t = ba.shape[0]
    half = LANES // 2
    return pl.pallas_call(
        _gdn_gates_kernel,
        grid=(t // ROW_TILE,),
        in_specs=[pl.BlockSpec((ROW_TILE, LANES), lambda i: (i, 0)),
                  pl.BlockSpec((1, half), lambda i: (0, 0)),
                  pl.BlockSpec((1, half), lambda i: (0, 0))],
        out_specs=[pl.BlockSpec((ROW_TILE, LANES), lambda i: (i, 0)),
                   pl.BlockSpec((LANES, ROW_TILE), lambda i: (0, i))],
        out_shape=[jax.ShapeDtypeStruct((t, LANES), F32), jax.ShapeDtypeStruct((LANES, t), F32)],
        compiler_params=_params("arbitrary"),
        name="gdn_gates",
    )(ba, a_log.reshape(1, half), dt_bias.reshape(1, half))


def _mxu_dot(a, b):
    return jnp.dot(a.astype(MXU_DTYPE), b.astype(MXU_DTYPE), preferred_element_type=F32)


def _emit_round_robin(programs):
    live = list(programs)
    while live:
        for prog in list(live):
            try:
                next(prog)
            except StopIteration:
                live.remove(prog)


def _gdn_scan_kernel(*refs, backward):
    if backward:
        q_ref, k_ref, v_ref, gb_ref, gbt_ref, fwd_ref, z_ref, nw_ref, o_ref, state_ref = refs
    else:
        q_ref, k_ref, v_ref, gb_ref, gbt_ref, o_ref, state_ref = refs
    n = ROW_TILE
    rep = GDN_V_HEADS // GDN_K_HEADS
    d = int(backward)

    @pl.when(pl.program_id(2) == 0)
    def _():
        state_ref[...] = jnp.zeros(state_ref.shape, F32)

    row = lax.broadcasted_iota(jnp.int32, (n, n), 0)
    col = lax.broadcasted_iota(jnp.int32, (n, n), 1)
    ahead = col - row if backward else row - col
    incl = ahead >= 0
    strict = ahead > 0
    block_xor = row ^ col
    last = 0 if backward else n - 1
    lane = lax.broadcasted_iota(jnp.int32, (n, LANES), 1)
    sub1 = lax.broadcasted_iota(jnp.int32, (n, 1), 0)
    gb = gb_ref[...]
    half = LANES // 2

    c = GDN_SOLVE_BLOCK
    log_c = int(np.log2(c))
    p_row = lax.broadcasted_iota(jnp.int32, (c, n), 0)
    p_lane = lax.broadcasted_iota(jnp.int32, (c, n), 1)
    p_blk = p_lane >> log_c
    p_xor = p_row ^ (p_lane & (c - 1))

    def pack(mat):
        out = mat[n - c:]
        for a in range(n // c - 2, -1, -1):
            out = jnp.where(p_blk == a, mat[a * c:(a + 1) * c], out)
        return out

    def unpack(packed):
        return jnp.concatenate([jnp.where(p_blk == a, packed, 0.0) for a in range(n // c)], axis=0)

    def key_head_program(kh):
        q = q_ref[:, kh * GDN_DK:(kh + 1) * GDN_DK]
        k = k_ref[:, kh * GDN_DK:(kh + 1) * GDN_DK]
        gram = _nt_dot(jnp.concatenate([q, k], axis=0), k)
        qk, kk = gram[:n], gram[n:]
        probs = []
        for e in range(rep):
            idx = kh * rep + e
            c_beta = d * GDN_V_HEADS + (pl.program_id(1) * GDN_SCAN_K_HEADS + kh) * rep + e
            c_g = half + c_beta
            beta_c = jnp.sum(jnp.where(lane == c_beta, gb, 0.0), axis=-1, keepdims=True)
            g_c = jnp.sum(jnp.where(lane == c_g, gb, 0.0), axis=-1, keepdims=True)
            beta_r = gbt_ref[pl.ds(c_beta, 1), :]
            g_r = gbt_ref[pl.ds(c_g, 1), :]
            g_last = jnp.sum(jnp.where(sub1 == last, g_c, 0.0), axis=0, keepdims=True)
            decay = jnp.exp2(jnp.where(incl, g_c - g_r, -jnp.inf))
            probs.append(dict(
                idx=idx, v=v_ref[:, idx * GDN_DV:(idx + 1) * GDN_DV], g_c=g_c, g_last=g_last,
                beta_r=beta_r, qk_dec=qk * decay, a_mat=jnp.where(strict, beta_c * kk * decay, 0.0)))
        yield
        packed_a = [pack(p["a_mat"]) for p in probs]
        packed_t = [(p_xor == 0).astype(F32) - jnp.where(p_xor == 1, pa, 0.0) for pa in packed_a]
        for log_m in range(1, log_c):
            off = (p_xor >> log_m) == 1
            xs = [_mxu_dot(jnp.where(off, pa, 0.0), unpack(pt)) for pa, pt in zip(packed_a, packed_t)]
            yield
            packed_t = [pt - _mxu_dot(pt, unpack(x)) for pt, x in zip(packed_t, xs)]
            yield
        invs = [unpack(pt) for pt in packed_t]
        for log_m in range(log_c, int(np.log2(n))):
            off = (block_xor >> log_m) == 1
            xs = [_mxu_dot(jnp.where(off, p["a_mat"], 0.0), inv) for p, inv in zip(probs, invs)]
            yield
            invs = [inv - _mxu_dot(inv, x) for inv, x in zip(invs, xs)]
            yield
        egs = [jnp.exp2(p["g_c"]) for p in probs]
        uws = [_mxu_dot(inv * p["beta_r"], jnp.concatenate([p["v"].astype(F32), k.astype(F32) * eg], axis=1))
               for p, inv, eg in zip(probs, invs, egs)]
        yield
        states = [state_ref[p["idx"]] for p in probs]
        from_state = [_mxu_dot(jnp.concatenate([uw[:, GDN_DV:], q.astype(F32) * eg], axis=0), s)
                      for uw, eg, s in zip(uws, egs, states)]
        yield
        v_news = [uw[:, :GDN_DV] - r[:n] for uw, r in zip(uws, from_state)]
        outs = [r[n:] + _mxu_dot(p["qk_dec"], v_new) for p, r, v_new in zip(probs, from_state, v_news)]
        yield
        for p, s, v_new, o in zip(probs, states, v_news, outs):
            i = p["idx"]
            cols = slice(i * GDN_DV, (i + 1) * GDN_DV)
            if backward:
                o = o + fwd_ref[:, cols]
                y = o * lax.rsqrt(jnp.mean(o * o, axis=-1, keepdims=True) + EPS) * nw_ref[...]
                o_ref[:, cols] = (y * _silu(z_ref[:, cols])).astype(o_ref.dtype)
            else:
                o_ref[:, cols] = o
            k_dec = k.astype(F32) * jnp.exp2(p["g_last"] - p["g_c"])
            state_ref[i] = s * jnp.exp2(p["g_last"]) + lax.dot_general(
                k_dec.astype(MXU_DTYPE), v_new.astype(MXU_DTYPE), (((0,), (0,)), ((), ())),
                preferred_element_type=F32)

    _emit_round_robin([key_head_program(kh) for kh in range(GDN_SCAN_K_HEADS)])


def _gdn_scan(qkv, gb, gbt, proj, o_norm, geom):
    t = qkv.shape[0]
    rep = GDN_V_HEADS // GDN_K_HEADS
    kw = GDN_SCAN_K_HEADS * GDN_DK
    vw = GDN_SCAN_K_HEADS * rep * GDN_DV
    grid = (geom.batch, GDN_K_HEADS // GDN_SCAN_K_HEADS, geom.n_ctx + geom.n_lat)
    scratch = [pltpu.VMEM((GDN_SCAN_K_HEADS * rep, GDN_DK, GDN_DV), F32)]
    sem = _params("arbitrary", "arbitrary", "arbitrary")

    def specs(backward):
        def tile(b, i):
            ctx_pos = geom.n_ctx - 1 - i if backward else i
            lat_pos = geom.n_lat - 1 - (i - geom.n_ctx) if backward else i - geom.n_ctx
            return jnp.where(i < geom.n_ctx, geom.lat_tiles + b * geom.n_ctx + ctx_pos, b * geom.n_lat + lat_pos)

        ins = [pl.BlockSpec((ROW_TILE, kw), lambda b, h, i: (tile(b, i), h)),
               pl.BlockSpec((ROW_TILE, kw), lambda b, h, i: (tile(b, i), GDN_KEY // kw + h)),
               pl.BlockSpec((ROW_TILE, vw), lambda b, h, i: (tile(b, i), 2 * GDN_KEY // vw + h)),
               pl.BlockSpec((ROW_TILE, LANES), lambda b, h, i: (tile(b, i), 0)),
               pl.BlockSpec((LANES, ROW_TILE), lambda b, h, i: (0, tile(b, i)))]
        out = pl.BlockSpec((ROW_TILE, vw), lambda b, h, i: (tile(b, i), h))
        z = pl.BlockSpec((ROW_TILE, vw), lambda b, h, i: (tile(b, i), GDN_CONV_CH // vw + h))
        return ins, out, z

    ins, out, _ = specs(False)
    o_fwd = pl.pallas_call(
        functools.partial(_gdn_scan_kernel, backward=False), grid=grid, in_specs=ins, out_specs=out,
        out_shape=jax.ShapeDtypeStruct((t, GDN_VAL), F32), scratch_shapes=scratch, compiler_params=sem,
        name="gdn_scan_fwd",
    )(qkv, qkv, qkv, gb, gbt)
    ins, out, z = specs(True)
    return pl.pallas_call(
        functools.partial(_gdn_scan_kernel, backward=True), grid=grid,
        in_specs=ins + [out, z, pl.BlockSpec((1, GDN_DV), lambda b, h, i: (0, 0))], out_specs=out,
        out_shape=jax.ShapeDtypeStruct((t, GDN_VAL), MXU_DTYPE), scratch_shapes=scratch, compiler_params=sem,
        name="gdn_scan_bwd",
    )(qkv, qkv, qkv, gb, gbt, o_fwd, proj, o_norm.reshape(1, GDN_DV))


def _final_norm_kernel(x_ref, w_ref, o_ref):
    x = x_ref[...]
    o_ref[...] = x * lax.rsqrt(jnp.mean(x * x, axis=-1, keepdims=True) + EPS) * w_ref[...]


def _final_norm(h, w, geom):
    d = h.shape[1]
    return pl.pallas_call(
        _final_norm_kernel,
        grid=(geom.lat_tiles,),
        in_specs=[pl.BlockSpec((ROW_TILE, d), lambda i: (i, 0)),
                  pl.BlockSpec((1, d), lambda i: (0, 0))],
        out_specs=pl.BlockSpec((ROW_TILE, d), lambda i: (i, 0)),
        out_shape=jax.ShapeDtypeStruct((geom.lat_tiles * ROW_TILE, d), F32),
        compiler_params=_params("arbitrary"),
        name="final_norm",
    )(h, w.reshape(1, d))


def _rope_table(batch, ctx_len, seq):
    pos = jnp.arange(seq)
    r = (pos // GRID_W).astype(F32)
    col = (pos % GRID_W).astype(F32)
    half = MLA_ROPE // 2
    inv = ROPE_THETA ** (-jnp.arange(0, half, 2, dtype=F32) / half)
    ang = jnp.concatenate([r[:, None] * inv, col[:, None] * inv], axis=-1)
    cos, sin = jnp.cos(ang), jnp.sin(ang)
    lat = jnp.concatenate([cos, cos, -sin, sin], axis=-1)
    ident = jnp.concatenate([jnp.ones((ctx_len, MLA_ROPE), F32), jnp.zeros((ctx_len, MLA_ROPE), F32)], axis=-1)
    return jnp.concatenate([jnp.tile(lat, (batch, 1)), jnp.tile(ident, (batch, 1))], axis=0)


_ROPE_SWAP = np.concatenate([np.arange(MLA_ROPE // 2, MLA_ROPE), np.arange(MLA_ROPE // 2)])


def _mla_weights(w_in, w_uq, w_ukv):
    lo = Q_LORA + KV_LORA
    k_rope = w_in[:, lo:lo + MLA_ROPE]
    w_in_p = jnp.concatenate([w_in[:, :lo], w_in[:, lo + MLA_ROPE:], k_rope, k_rope[:, _ROPE_SWAP]], axis=1)
    head = np.concatenate([np.arange(MLA_QK), MLA_NOPE + _ROPE_SWAP])
    cols = (np.arange(MLA_HEADS)[:, None] * MLA_QK + head[None, :]).reshape(-1)
    wqt = w_uq[:, cols].T.astype(MXU_DTYPE)
    w_kv = w_ukv.reshape(KV_LORA, MLA_HEADS, MLA_NOPE + MLA_V)
    wk = w_kv[:, :, :MLA_NOPE].reshape(KV_LORA, MLA_HEADS * MLA_NOPE).astype(MXU_DTYPE)
    wvt = w_kv[:, :, MLA_NOPE:].reshape(KV_LORA, MLA_WIDTH).T.astype(MXU_DTYPE)
    return w_in_p, wqt, wk, wvt


def _mla_layer(h, a, mod, tab, tabt, w_in, q_norm, w_uq, kv_norm, w_ukv, w_o_all, layer, geom):
    w_in_p, wqt, wk, wvt = _mla_weights(w_in, w_uq, w_ukv)
    proj = _matmul(a, w_in_p[None], 0, tn=640, name="mla_in")
    qt, k, vt = _mla_up(proj, q_norm, kv_norm, wqt, wk, wvt, tab, tabt)
    gated = _attention(qt, k, vt, proj, geom)
    return _matmul_residual(gated, w_o_all, layer, h, mod, geom, name="mla_out")


def _gdn_layer(h, a, mod, w_in_all, conv_w, a_log, dt_bias, o_norm, w_o_all, layer, geom):
    proj = _matmul(a, w_in_all, layer, tn=1024, n_blocks=GDN_QKVZ // 1024, name="gdn_in")
    ba = _matmul(a, w_in_all, layer, tn=LANES, first_block=GDN_QKVZ // LANES, n_blocks=1, name="gdn_in_gates")
    qkv = _gdn_conv(proj, conv_w, geom)
    gb, gbt = _gdn_gates(ba, a_log, dt_bias)
    gated = _gdn_scan(qkv, gb, gbt, proj, o_norm, geom)
    return _matmul_residual(gated, w_o_all, layer, h, mod, geom, name="gdn_out")


def kernel(x, c, ctx, c_ctx, ada_w, ada_b, norm_w, mla_w_in, mla_q_norm, mla_w_uq, mla_kv_norm, mla_w_ukv, mla_w_o, gdn_w_in, gdn_conv_w, gdn_a_log, gdn_dt_bias, gdn_o_norm, gdn_w_o, final_norm):
    batch, seq, d = x.shape
    ctx_len = ctx.shape[1]
    n_rows = batch * (seq + ctx_len)
    assert d == D_MODEL and seq % ATTN_TQ == 0 and ctx_len % ROW_TILE == 0 and seq % GRID_W == 0
    assert batch < SUBLANES and n_rows % MM_TM == 0 and (batch * seq) % ctx_len == 0
    geom = Geom(batch=batch, n_lat=seq // ROW_TILE, n_ctx=ctx_len // ROW_TILE)

    h = jnp.concatenate([x.reshape(batch * seq, d), ctx.reshape(batch * ctx_len, d)], axis=0)
    conds = jnp.zeros((SUBLANES, d), F32).at[:batch].set(c).at[batch].set(c_ctx)
    mods = _adaln(conds, ada_w, ada_b)
    tab = _rope_table(batch, ctx_len, seq)
    tabt = tab.T

    for i in range(DEPTH):
        a = _prenorm(h, norm_w[i], mods[i], geom)
        j = i // 2
        if i % 2 == 0:
            h = _mla_layer(h, a, mods[i], tab, tabt, mla_w_in[j], mla_q_norm[j], mla_w_uq[j], mla_kv_norm[j],
                           mla_w_ukv[j], mla_w_o, j, geom)
        else:
            h = _gdn_layer(h, a, mods[i], gdn_w_in, gdn_conv_w[j], gdn_a_log[j], gdn_dt_bias[j],
                           gdn_o_norm[j], gdn_w_o, j, geom)
    return _final_norm(h, final_norm, geom).reshape(batch, seq, d)
```

```python
import functools
import math
from typing import NamedTuple

import jax
import jax.numpy as jnp
import numpy as np
from jax import lax
from jax.experimental import pallas as pl
from jax.experimental.pallas import tpu as pltpu

F32 = jnp.float32
MXU_DTYPE = jnp.bfloat16

D_MODEL = 2048
DEPTH = 4
EPS = 1e-6
GRID_W = 64
ROPE_THETA = 10000.0
MLA_HEADS = 16
MLA_NOPE = 128
MLA_ROPE = 64
MLA_V = 128
MLA_QK = MLA_NOPE + MLA_ROPE
Q_LORA = 512
KV_LORA = 512
MLA_WIDTH = MLA_HEADS * MLA_V
MLA_HEAD_COLS = 256
MLA_VT_ROWS = MLA_V + 16
MLA_IN_PAD = Q_LORA + KV_LORA + MLA_WIDTH + 2 * MLA_ROPE
GDN_K_HEADS = 16
GDN_V_HEADS = 32
GDN_DK = 128
GDN_DV = 128
GDN_KEY = GDN_K_HEADS * GDN_DK
GDN_VAL = GDN_V_HEADS * GDN_DV
GDN_CONV_CH = 2 * GDN_KEY + GDN_VAL
GDN_QKVZ = GDN_CONV_CH + GDN_VAL
CONV_W = 5

LANES = 128
SUBLANES = 8
ROW_TILE = 256
MM_TM = 512
MM_TM_MAX = 1100
GDN_CONV_TC = 2048
ATTN_TQ = 2048
ATTN_TK = 512
GDN_SCAN_K_HEADS = 4
GDN_SOLVE_BLOCK = 64
VMEM_LIMIT_BYTES = 50 * 1024 * 1024


class Geom(NamedTuple):
    batch: int
    n_lat: int
    n_ctx: int

    @property
    def lat_tiles(self):
        return self.batch * self.n_lat

    def mod_row(self, tile):
        return jnp.where(tile < self.lat_tiles, tile // self.n_lat, self.batch)

    def seq_pos(self, tile):
        is_lat = tile < self.lat_tiles
        pos = jnp.where(is_lat, tile % self.n_lat, (tile - self.lat_tiles) % self.n_ctx)
        return pos, jnp.where(is_lat, self.n_lat, self.n_ctx)


def _params(*sem):
    return pltpu.CompilerParams(dimension_semantics=sem, vmem_limit_bytes=VMEM_LIMIT_BYTES)


def _sigmoid(x):
    return 1.0 / (1.0 + jnp.exp(-x))


def _silu(x):
    return x * _sigmoid(x)


def _nt_dot(a, b):
    return lax.dot_general(a, b, (((1,), (1,)), ((), ())), preferred_element_type=F32)


def _adaln_kernel(c_ref, w_ref, b_ref, o_ref):
    a = _silu(c_ref[...]).astype(MXU_DTYPE)
    w = w_ref[0].astype(MXU_DTYPE)
    o_ref[0] = jnp.dot(a, w, preferred_element_type=F32) + b_ref[0]


def _adaln(conds, ada_w, ada_b):
    depth, d, n = ada_w.shape
    tn = 768
    return pl.pallas_call(
        _adaln_kernel,
        grid=(depth, n // tn),
        in_specs=[pl.BlockSpec((SUBLANES, d), lambda l, j: (0, 0)),
                  pl.BlockSpec((1, d, tn), lambda l, j: (l, 0, j)),
                  pl.BlockSpec((1, 1, tn), lambda l, j: (l, 0, j))],
        out_specs=pl.BlockSpec((1, SUBLANES, tn), lambda l, j: (l, 0, j)),
        out_shape=jax.ShapeDtypeStruct((depth, SUBLANES, n), F32),
        compiler_params=_params("arbitrary", "arbitrary"),
        name="adaln",
    )(conds, ada_w, ada_b.reshape(depth, 1, n))


def _prenorm_kernel(x_ref, nw_ref, mod_ref, o_ref, *, geom):
    m = mod_ref[pl.ds(geom.mod_row(pl.program_id(0)), 1), :]
    x = x_ref[...]
    y = x * lax.rsqrt(jnp.mean(x * x, axis=-1, keepdims=True) + EPS) * nw_ref[...]
    o_ref[...] = (y * (1.0 + m[:, D_MODEL:2 * D_MODEL]) + m[:, :D_MODEL]).astype(o_ref.dtype)


def _prenorm(h, norm_w, mod, geom):
    t, d = h.shape
    return pl.pallas_call(
        functools.partial(_prenorm_kernel, geom=geom),
        grid=(t // ROW_TILE,),
        in_specs=[pl.BlockSpec((ROW_TILE, d), lambda i: (i, 0)),
                  pl.BlockSpec((1, d), lambda i: (0, 0)),
                  pl.BlockSpec((SUBLANES, 3 * d), lambda i: (0, 0))],
        out_specs=pl.BlockSpec((ROW_TILE, d), lambda i: (i, 0)),
        out_shape=jax.ShapeDtypeStruct((t, d), MXU_DTYPE),
        compiler_params=_params("arbitrary"),
        name="prenorm",
    )(h, norm_w.reshape(1, d), mod)


def _mm_kernel(a_ref, w_ref, o_ref, w_mxu_ref):
    @pl.when(pl.program_id(1) == 0)
    def _():
        w_mxu_ref[...] = w_ref[0].astype(MXU_DTYPE)

    o_ref[...] = jnp.dot(a_ref[...], w_mxu_ref[...], preferred_element_type=F32).astype(o_ref.dtype)


def _matmul(a, w, layer, *, tn, first_block=0, n_blocks=None, out_dtype=F32, name="matmul"):
    m, k = a.shape
    n_blocks = w.shape[2] // tn if n_blocks is None else n_blocks
    tm = max(t for t in range(16, MM_TM_MAX + 1, 16) if m % t == 0)
    return pl.pallas_call(
        _mm_kernel,
        grid=(n_blocks, m // tm),
        in_specs=[pl.BlockSpec((tm, k), lambda j, i: (i, 0)),
                  pl.BlockSpec((1, k, tn), lambda j, i: (layer, 0, first_block + j))],
        out_specs=pl.BlockSpec((tm, tn), lambda j, i: (i, j)),
        out_shape=jax.ShapeDtypeStruct((m, n_blocks * tn), out_dtype),
        scratch_shapes=[pltpu.VMEM((k, tn), MXU_DTYPE)],
        compiler_params=_params("arbitrary", "arbitrary"),
        name=name,
    )(a, w)


def _mm_res_kernel(a_ref, w_ref, res_ref, gate_ref, o_ref, w_mxu_ref, *, geom):
    @pl.when(pl.program_id(1) == 0)
    def _():
        w_mxu_ref[...] = w_ref[0].astype(MXU_DTYPE)

    acc = jnp.dot(a_ref[...], w_mxu_ref[...], preferred_element_type=F32)
    for s in range(MM_TM // ROW_TILE):
        tile = pl.program_id(1) * (MM_TM // ROW_TILE) + s
        g = gate_ref[pl.ds(geom.mod_row(tile), 1), :]
        rows = slice(s * ROW_TILE, (s + 1) * ROW_TILE)
        o_ref[rows, :] = res_ref[rows, :] + g * acc[rows, :]


def _matmul_residual(a, w, layer, res, mod, geom, *, tn=512, name="matmul_res"):
    m, k = a.shape
    n = w.shape[2]
    gate_block0 = 2 * D_MODEL // tn
    return pl.pallas_call(
        functools.partial(_mm_res_kernel, geom=geom),
        grid=(n // tn, m // MM_TM),
        in_specs=[pl.BlockSpec((MM_TM, k), lambda j, i: (i, 0)),
                  pl.BlockSpec((1, k, tn), lambda j, i: (layer, 0, j)),
                  pl.BlockSpec((MM_TM, tn), lambda j, i: (i, j)),
                  pl.BlockSpec((SUBLANES, tn), lambda j, i: (0, gate_block0 + j))],
        out_specs=pl.BlockSpec((MM_TM, tn), lambda j, i: (i, j)),
        out_shape=jax.ShapeDtypeStruct((m, n), F32),
        scratch_shapes=[pltpu.VMEM((k, tn), MXU_DTYPE)],
        compiler_params=_params("arbitrary", "arbitrary"),
        name=name,
    )(a, w, res, mod)


def _mla_up_kernel(cq_ref, ckv_ref, kr_ref, qn_ref, kvn_ref, wqt_ref, wk_ref, wvt_ref, tab_ref, tabt_ref,
                   qt_ref, k_ref, vt_ref):
    def rms(x, w):
        return (x * lax.rsqrt(jnp.mean(x * x, axis=-1, keepdims=True) + EPS) * w).astype(MXU_DTYPE)

    cq = rms(cq_ref[...], qn_ref[...])
    ckv = rms(ckv_ref[...], kvn_ref[...])
    tm = cq.shape[0]
    t = kr_ref[...] * tab_ref[...]
    lane = lax.broadcasted_iota(jnp.int32, t.shape, 1)
    k_rot = jnp.where(lane < MLA_ROPE, t + pltpu.roll(t, MLA_ROPE, axis=1), 0.0).astype(k_ref.dtype)
    tabt = tabt_ref[...]
    zeros = jnp.zeros((MLA_ROPE, tm), qt_ref.dtype)
    q_scale = MLA_QK ** -0.5 * math.log2(math.e)
    for h in range(MLA_HEADS):
        q0 = h * MLA_HEAD_COLS
        qh = _nt_dot(wqt_ref[q0:q0 + MLA_HEAD_COLS, :], cq) * q_scale
        qt_ref[q0:q0 + MLA_NOPE, :] = qh[:MLA_NOPE].astype(qt_ref.dtype)
        tq = qh[MLA_NOPE:] * tabt
        qt_ref[q0 + MLA_NOPE:q0 + MLA_NOPE + MLA_ROPE, :] = (tq[:MLA_ROPE] + tq[MLA_ROPE:]).astype(qt_ref.dtype)
        qt_ref[q0 + MLA_NOPE + MLA_ROPE:q0 + MLA_HEAD_COLS, :] = zeros
        k_ref[:, q0:q0 + MLA_NOPE] = jnp.dot(ckv, wk_ref[:, h * MLA_NOPE:(h + 1) * MLA_NOPE],
                                            preferred_element_type=F32).astype(k_ref.dtype)
        k_ref[:, q0 + MLA_NOPE:q0 + MLA_HEAD_COLS] = k_rot
        v0 = h * MLA_VT_ROWS
        vt_ref[v0:v0 + MLA_V, :] = _nt_dot(wvt_ref[h * MLA_V:(h + 1) * MLA_V, :], ckv).astype(vt_ref.dtype)
        vt_ref[v0 + MLA_V:v0 + MLA_VT_ROWS, :] = jnp.ones((MLA_VT_ROWS - MLA_V, tm), vt_ref.dtype)


def _mla_up(proj, q_norm, kv_norm, wqt, wk, wvt, tab, tabt):
    t = proj.shape[0]
    tm = ROW_TILE
    nq = MLA_HEADS * MLA_HEAD_COLS
    kr_block = (Q_LORA + KV_LORA + MLA_WIDTH) // LANES
    const = lambda i: (0, 0)
    return pl.pallas_call(
        _mla_up_kernel,
        grid=(t // tm,),
        in_specs=[pl.BlockSpec((tm, Q_LORA), lambda i: (i, 0)),
                  pl.BlockSpec((tm, KV_LORA), lambda i: (i, 1)),
                  pl.BlockSpec((tm, LANES), lambda i: (i, kr_block)),
                  pl.BlockSpec((1, Q_LORA), const),
                  pl.BlockSpec((1, KV_LORA), const),
                  pl.BlockSpec(wqt.shape, const),
                  pl.BlockSpec(wk.shape, const),
                  pl.BlockSpec(wvt.shape, const),
                  pl.BlockSpec((tm, LANES), lambda i: (i, 0)),
                  pl.BlockSpec((LANES, tm), lambda i: (0, i))],
        out_specs=[pl.BlockSpec((nq, tm), lambda i: (0, i)),
                   pl.BlockSpec((tm, nq), lambda i: (i, 0)),
                   pl.BlockSpec((MLA_HEADS * MLA_VT_ROWS, tm), lambda i: (0, i))],
        out_shape=[jax.ShapeDtypeStruct((nq, t), MXU_DTYPE),
                   jax.ShapeDtypeStruct((t, nq), MXU_DTYPE),
                   jax.ShapeDtypeStruct((MLA_HEADS * MLA_VT_ROWS, t), MXU_DTYPE)],
        compiler_params=_params("arbitrary"),
        name="mla_up",
    )(proj, proj, proj, q_norm.reshape(1, -1), kv_norm.reshape(1, -1), wqt, wk, wvt, tab, tabt)


def _attn_kernel(*refs, n_segments, tk):
    qt_ref = refs[0]
    k_refs = refs[1:1 + n_segments]
    vt_refs = refs[1 + n_segments:1 + 2 * n_segments]
    z_ref, o_ref = refs[1 + 2 * n_segments:]
    qt = qt_ref[...]
    tq = qt.shape[1]
    m = jnp.full((1, tq), -jnp.inf, F32)
    acc = jnp.zeros((MLA_VT_ROWS, tq), F32)
    tiles = [(k_ref, vt_ref, off, min(tk, k_ref.shape[0])) for k_ref, vt_ref in zip(k_refs, vt_refs)
             for off in range(0, k_ref.shape[0], min(tk, k_ref.shape[0]))]

    def scores(tile):
        k_ref, _, off, rows = tile
        return jnp.dot(k_ref[off:off + rows, :], qt, preferred_element_type=F32)

    s_next = scores(tiles[0])
    for n, (_, vt_ref, off, tk) in enumerate(tiles):
        s = s_next
        if n + 1 < len(tiles):
            s_next = scores(tiles[n + 1])
        m_new = jnp.maximum(m, jnp.max(s, axis=0, keepdims=True))
        alpha = jnp.exp2(m - m_new)
        p = jnp.exp2(s - m_new)
        acc = alpha * acc + jnp.dot(vt_ref[:, off:off + tk], p.astype(MXU_DTYPE), preferred_element_type=F32)
        m = m_new
    o = (acc[:MLA_V] / acc[MLA_V:MLA_V + 1]).T
    o_ref[...] = (o * _silu(z_ref[...])).astype(o_ref.dtype)


def _attention(qt, k, vt, proj, geom):
    seq, ctx_len = geom.n_lat * ROW_TILE, geom.n_ctx * ROW_TILE
    ctx_block0 = geom.batch * seq // ctx_len
    z_block0 = (Q_LORA + KV_LORA) // LANES
    k_lat = pl.BlockSpec((seq, MLA_HEAD_COLS), lambda b, h, i: (b, h))
    k_ctx = pl.BlockSpec((ctx_len, MLA_HEAD_COLS), lambda b, h, i: (ctx_block0 + b, h))
    vt_lat = pl.BlockSpec((MLA_VT_ROWS, seq), lambda b, h, i: (h, b))
    vt_ctx = pl.BlockSpec((MLA_VT_ROWS, ctx_len), lambda b, h, i: (h, ctx_block0 + b))
    sem = _params("arbitrary", "arbitrary", "arbitrary")

    tq = ATTN_TQ
    nq = seq // tq
    lat = pl.pallas_call(
        functools.partial(_attn_kernel, n_segments=2, tk=ATTN_TK),
        grid=(geom.batch, MLA_HEADS, nq),
        in_specs=[pl.BlockSpec((MLA_HEAD_COLS, tq), lambda b, h, i: (h, b * nq + i)),
                  k_ctx, k_lat, vt_ctx, vt_lat,
                  pl.BlockSpec((tq, MLA_V), lambda b, h, i: (b * nq + i, z_block0 + h))],
        out_specs=pl.BlockSpec((tq, MLA_V), lambda b, h, i: (b * nq + i, h)),
        out_shape=jax.ShapeDtypeStruct((geom.batch * seq, MLA_WIDTH), MXU_DTYPE),
        compiler_params=sem, name="mla_attention",
    )(qt, k, k, vt, vt, proj)
    ctx = pl.pallas_call(
        functools.partial(_attn_kernel, n_segments=1, tk=ATTN_TK),
        grid=(geom.batch, MLA_HEADS, 1),
        in_specs=[pl.BlockSpec((MLA_HEAD_COLS, ctx_len), lambda b, h, i: (h, ctx_block0 + b)),
                  k_ctx, vt_ctx,
                  pl.BlockSpec((ctx_len, MLA_V), lambda b, h, i: (ctx_block0 + b, z_block0 + h))],
        out_specs=pl.BlockSpec((ctx_len, MLA_V), lambda b, h, i: (b, h)),
        out_shape=jax.ShapeDtypeStruct((geom.batch * ctx_len, MLA_WIDTH), MXU_DTYPE),
        compiler_params=sem, name="mla_attention_ctx",
    )(qt, k, vt, proj)
    return jnp.concatenate([lat, ctx], axis=0)


def _gdn_conv_kernel(cur_ref, prev_ref, next_ref, w_ref, o_ref, pad_ref, *, geom, tc):
    c = pl.program_id(1)
    pos, n_tiles = geom.seq_pos(pl.program_id(0))
    pad_ref[0:SUBLANES, :] = jnp.where(pos != 0, prev_ref[...], 0.0)
    pad_ref[SUBLANES:SUBLANES + ROW_TILE, :] = cur_ref[...]
    pad_ref[SUBLANES + ROW_TILE:, :] = jnp.where(pos != n_tiles - 1, next_ref[...], 0.0)
    w = w_ref[...]
    n_pad = pad_ref.shape[0]
    key_tiles = GDN_KEY // tc
    is_qk = c < 2 * key_tiles
    scale = jnp.where(c < key_tiles, GDN_DK ** -0.5, 1.0)
    for g in range(tc // GDN_DK):
        cols = slice(g * GDN_DK, (g + 1) * GDN_DK)
        padded = pad_ref[:, cols]
        y = jnp.zeros((ROW_TILE, GDN_DK), F32)
        for k in range(CONV_W):
            shift = (CONV_W // 2 - k) % n_pad
            taps = padded if shift == 0 else pltpu.roll(padded, shift, axis=0)
            y = y + taps[SUBLANES:SUBLANES + ROW_TILE] * w[k:k + 1, cols]
        y = _silu(y)
        inv = lax.rsqrt(jnp.sum(y * y, axis=-1, keepdims=True) + EPS) * scale
        o_ref[:, cols] = (y * jnp.where(is_qk, inv, 1.0)).astype(o_ref.dtype)


def _gdn_conv(proj, conv_w, geom):
    t = proj.shape[0]
    tc = GDN_CONV_TC
    halo_per_tile = ROW_TILE // SUBLANES
    last_halo = t // SUBLANES - 1
    return pl.pallas_call(
        functools.partial(_gdn_conv_kernel, geom=geom, tc=tc),
        grid=(t // ROW_TILE, GDN_CONV_CH // tc),
        in_specs=[pl.BlockSpec((ROW_TILE, tc), lambda i, c: (i, c)),
                  pl.BlockSpec((SUBLANES, tc), lambda i, c: (jnp.maximum(i * halo_per_tile - 1, 0), c)),
                  pl.BlockSpec((SUBLANES, tc), lambda i, c: (jnp.minimum((i + 1) * halo_per_tile, last_halo), c)),
                  pl.BlockSpec((CONV_W, tc), lambda i, c: (0, c))],
        out_specs=pl.BlockSpec((ROW_TILE, tc), lambda i, c: (i, c)),
        out_shape=jax.ShapeDtypeStruct((t, GDN_CONV_CH), MXU_DTYPE),
        scratch_shapes=[pltpu.VMEM((ROW_TILE + 2 * SUBLANES, tc), F32)],
        compiler_params=_params("arbitrary", "arbitrary"),
        name="gdn_conv",
    )(proj, proj, proj, conv_w)


def _split3(x):
    hi = x.astype(MXU_DTYPE)
    r = x - hi.astype(F32)
    mid = r.astype(MXU_DTYPE)
    lo = (r - mid.astype(F32)).astype(MXU_DTYPE)
    return hi, mid, lo


def _gdn_gates_kernel(ba_ref, alog_ref, dtb_ref, gb_ref, gbt_ref):
    half = LANES // 2
    ba = ba_ref[...]
    beta = _sigmoid(ba[:, :half])
    x = ba[:, half:] + dtb_ref[...]
    softplus = jnp.maximum(x, 0.0) + jnp.log(1.0 + jnp.exp(-jnp.abs(x)))
    g = -jnp.exp(alog_ref[...]) * softplus * math.log2(math.e)
    row = lax.broadcasted_iota(jnp.int32, (ROW_TILE, ROW_TILE), 0)
    col = lax.broadcasted_iota(jnp.int32, (ROW_TILE, ROW_TILE), 1)
    lower = (row >= col).astype(MXU_DTYPE)
    upper = (row <= col).astype(MXU_DTYPE)
    pre = jnp.zeros((ROW_TILE, half), F32)
    suf = jnp.zeros((ROW_TILE, half), F32)
    for piece in _split3(g):
        pre = pre + jnp.dot(lower, piece, preferred_element_type=F32)
        suf = suf + jnp.dot(upper, piece, preferred_element_type=F32)
    lane = lax.broadcasted_iota(jnp.int32, (ROW_TILE, half), 1)
    gcum = jnp.where(lane < half // 2, pre, suf)
    gb = jnp.concatenate([beta, gcum], axis=-1)
    gb_ref[...] = gb
    gbt_ref[...] = gb.T


def _gdn_gates(ba, a_log, dt_bias):
    t = ba.shape[0]
    half = LANES // 2
    return pl.pallas_call(
        _gdn_gates_kernel,
        grid=(t // ROW_TILE,),
        in_specs=[pl.BlockSpec((ROW_TILE, LANES), lambda i: (i, 0)),
                  pl.BlockSpec((1, half), lambda i: (0, 0)),
                  pl.BlockSpec((1, half), lambda i: (0, 0))],
        out_specs=[pl.BlockSpec((ROW_TILE, LANES), lambda i: (i, 0)),
                   pl.BlockSpec((LANES, ROW_TILE), lambda i: (0, i))],
        out_shape=[jax.ShapeDtypeStruct((t, LANES), F32), jax.ShapeDtypeStruct((LANES, t), F32)],
        compiler_params=_params("arbitrary"),
        name="gdn_gates",
    )(ba, a_log.reshape(1, half), dt_bias.reshape(1, half))


def _mxu_dot(a, b):
    return jnp.dot(a.astype(MXU_DTYPE), b.astype(MXU_DTYPE), preferred_element_type=F32)


def _emit_round_robin(programs):
    live = list(programs)
    while live:
        for prog in list(live):
            try:
                next(prog)
            except StopIteration:
                live.remove(prog)


def _gdn_scan_kernel(*refs, backward):
    if backward:
        q_ref, k_ref, v_ref, gb_ref, gbt_ref, fwd_ref, z_ref, nw_ref, o_ref, state_ref = refs
    else:
        q_ref, k_ref, v_ref, gb_ref, gbt_ref, o_ref, state_ref = refs
    n = ROW_TILE
    rep = GDN_V_HEADS // GDN_K_HEADS
    d = int(backward)

    @pl.when(pl.program_id(2) == 0)
    def _():
        state_ref[...] = jnp.zeros(state_ref.shape, F32)

    row = lax.broadcasted_iota(jnp.int32, (n, n), 0)
    col = lax.broadcasted_iota(jnp.int32, (n, n), 1)
    ahead = col - row if backward else row - col
    incl = ahead >= 0
    strict = ahead > 0
    block_xor = row ^ col
    last = 0 if backward else n - 1
    lane = lax.broadcasted_iota(jnp.int32, (n, LANES), 1)
    sub1 = lax.broadcasted_iota(jnp.int32, (n, 1), 0)
    gb = gb_ref[...]
    half = LANES // 2

    c = GDN_SOLVE_BLOCK
    log_c = int(np.log2(c))
    p_row = lax.broadcasted_iota(jnp.int32, (c, n), 0)
    p_lane = lax.broadcasted_iota(jnp.int32, (c, n), 1)
    p_blk = p_lane >> log_c
    p_xor = p_row ^ (p_lane & (c - 1))

    def pack(mat):
        out = mat[n - c:]
        for a in range(n // c - 2, -1, -1):
            out = jnp.where(p_blk == a, mat[a * c:(a + 1) * c], out)
        return out

    def unpack(packed):
        return jnp.concatenate([jnp.where(p_blk == a, packed, 0.0) for a in range(n // c)], axis=0)

    def key_head_program(kh):
        q = q_ref[:, kh * GDN_DK:(kh + 1) * GDN_DK]
        k = k_ref[:, kh * GDN_DK:(kh + 1) * GDN_DK]
        gram = _nt_dot(jnp.concatenate([q, k], axis=0), k)
        qk, kk = gram[:n], gram[n:]
        probs = []
        for e in range(rep):
            idx = kh * rep + e
            c_beta = d * GDN_V_HEADS + (pl.program_id(1) * GDN_SCAN_K_HEADS + kh) * rep + e
            c_g = half + c_beta
            beta_c = jnp.sum(jnp.where(lane == c_beta, gb, 0.0), axis=-1, keepdims=True)
            g_c = jnp.sum(jnp.where(lane == c_g, gb, 0.0), axis=-1, keepdims=True)
            beta_r = gbt_ref[pl.ds(c_beta, 1), :]
            g_r = gbt_ref[pl.ds(c_g, 1), :]
            g_last = jnp.sum(jnp.where(sub1 == last, g_c, 0.0), axis=0, keepdims=True)
            decay = jnp.exp2(jnp.where(incl, g_c - g_r, -jnp.inf))
            probs.append(dict(
                idx=idx, v=v_ref[:, idx * GDN_DV:(idx + 1) * GDN_DV], g_c=g_c, g_last=g_last,
                beta_r=beta_r, qk_dec=qk * decay, a_mat=jnp.where(strict, beta_c * kk * decay, 0.0)))
        yield
        packed_a = [pack(p["a_mat"]) for p in probs]
        packed_t = [(p_xor == 0).astype(F32) - jnp.where(p_xor == 1, pa, 0.0) for pa in packed_a]
        for log_m in range(1, log_c):
            off = (p_xor >> log_m) == 1
            xs = [_mxu_dot(jnp.where(off, pa, 0.0), unpack(pt)) for pa, pt in zip(packed_a, packed_t)]
            yield
            packed_t = [pt - _mxu_dot(pt, unpack(x)) for pt, x in zip(packed_t, xs)]
            yield
        invs = [unpack(pt) for pt in packed_t]
        for log_m in range(log_c, int(np.log2(n))):
            off = (block_xor >> log_m) == 1
            xs = [_mxu_dot(jnp.where(off, p["a_mat"], 0.0), inv) for p, inv in zip(probs, invs)]
            yield
            invs = [inv - _mxu_dot(inv, x) for inv, x in zip(invs, xs)]
            yield
        egs = [jnp.exp2(p["g_c"]) for p in probs]
        uws = [_mxu_dot(inv * p["beta_r"], jnp.concatenate([p["v"].astype(F32), k.astype(F32) * eg], axis=1))
               for p, inv, eg in zip(probs, invs, egs)]
        yield
        states = [state_ref[p["idx"]] for p in probs]
        from_state = [_mxu_dot(jnp.concatenate([uw[:, GDN_DV:], q.astype(F32) * eg], axis=0), s)
                      for uw, eg, s in zip(uws, egs, states)]
        yield
        v_news = [uw[:, :GDN_DV] - r[:n] for uw, r in zip(uws, from_state)]
        outs = [r[n:] + _mxu_dot(p["qk_dec"], v_new) for p, r, v_new in zip(probs, from_state, v_news)]
        yield
        for p, s, v_new, o in zip(probs, states, v_news, outs):
            i = p["idx"]
            cols = slice(i * GDN_DV, (i + 1) * GDN_DV)
            if backward:
                o = o + fwd_ref[:, cols]
                y = o * lax.rsqrt(jnp.mean(o * o, axis=-1, keepdims=True) + EPS) * nw_ref[...]
                o_ref[:, cols] = (y * _silu(z_ref[:, cols])).astype(o_ref.dtype)
            else:
                o_ref[:, cols] = o
            k_dec = k.astype(F32) * jnp.exp2(p["g_last"] - p["g_c"])
            state_ref[i] = s * jnp.exp2(p["g_last"]) + lax.dot_general(
                k_dec.astype(MXU_DTYPE), v_new.astype(MXU_DTYPE), (((0,), (0,)), ((), ())),
                preferred_element_type=F32)

    _emit_round_robin([key_head_program(kh) for kh in range(GDN_SCAN_K_HEADS)])


def _gdn_scan(qkv, gb, gbt, proj, o_norm, geom):
    t = qkv.shape[0]
    rep = GDN_V_HEADS // GDN_K_HEADS
    kw = GDN_SCAN_K_HEADS * GDN_DK
    vw = GDN_SCAN_K_HEADS * rep * GDN_DV
    grid = (geom.batch, GDN_K_HEADS // GDN_SCAN_K_HEADS, geom.n_ctx + geom.n_lat)
    scratch = [pltpu.VMEM((GDN_SCAN_K_HEADS * rep, GDN_DK, GDN_DV), F32)]
    sem = _params("arbitrary", "arbitrary", "arbitrary")

    def specs(backward):
        def tile(b, i):
            ctx_pos = geom.n_ctx - 1 - i if backward else i
            lat_pos = geom.n_lat - 1 - (i - geom.n_ctx) if backward else i - geom.n_ctx
            return jnp.where(i < geom.n_ctx, geom.lat_tiles + b * geom.n_ctx + ctx_pos, b * geom.n_lat + lat_pos)

        ins = [pl.BlockSpec((ROW_TILE, kw), lambda b, h, i: (tile(b, i), h)),
               pl.BlockSpec((ROW_TILE, kw), lambda b, h, i: (tile(b, i), GDN_KEY // kw + h)),
               pl.BlockSpec((ROW_TILE, vw), lambda b, h, i: (tile(b, i), 2 * GDN_KEY // vw + h)),
               pl.BlockSpec((ROW_TILE, LANES), lambda b, h, i: (tile(b, i), 0)),
               pl.BlockSpec((LANES, ROW_TILE), lambda b, h, i: (0, tile(b, i)))]
        out = pl.BlockSpec((ROW_TILE, vw), lambda b, h, i: (tile(b, i), h))
        z = pl.BlockSpec((ROW_TILE, vw), lambda b, h, i: (tile(b, i), GDN_CONV_CH // vw + h))
        return ins, out, z

    ins, out, _ = specs(False)
    o_fwd = pl.pallas_call(
        functools.partial(_gdn_scan_kernel, backward=False), grid=grid, in_specs=ins, out_specs=out,
        out_shape=jax.ShapeDtypeStruct((t, GDN_VAL), F32), scratch_shapes=scratch, compiler_params=sem,
        name="gdn_scan_fwd",
    )(qkv, qkv, qkv, gb, gbt)
    ins, out, z = specs(True)
    return pl.pallas_call(
        functools.partial(_gdn_scan_kernel, backward=True), grid=grid,
        in_specs=ins + [out, z, pl.BlockSpec((1, GDN_DV), lambda b, h, i: (0, 0))], out_specs=out,
        out_shape=jax.ShapeDtypeStruct((t, GDN_VAL), MXU_DTYPE), scratch_shapes=scratch, compiler_params=sem,
        name="gdn_scan_bwd",
    )(qkv, qkv, qkv, gb, gbt, o_fwd, proj, o_norm.reshape(1, GDN_DV))


def _final_norm_kernel(x_ref, w_ref, o_ref):
    x = x_ref[...]
    o_ref[...] = x * lax.rsqrt(jnp.mean(x * x, axis=-1, keepdims=True) + EPS) * w_ref[...]


def _final_norm(h, w, geom):
    d = h.shape[1]
    return pl.pallas_call(
        _final_norm_kernel,
        grid=(geom.lat_tiles,),
        in_specs=[pl.BlockSpec((ROW_TILE, d), lambda i: (i, 0)),
                  pl.BlockSpec((1, d), lambda i: (0, 0))],
        out_specs=pl.BlockSpec((ROW_TILE, d), lambda i: (i, 0)),
        out_shape=jax.ShapeDtypeStruct((geom.lat_tiles * ROW_TILE, d), F32),
        compiler_params=_params("arbitrary"),
        name="final_norm",
    )(h, w.reshape(1, d))


def _rope_table(batch, ctx_len, seq):
    pos = jnp.arange(seq)
    r = (pos // GRID_W).astype(F32)
    col = (pos % GRID_W).astype(F32)
    half = MLA_ROPE // 2
    inv = ROPE_THETA ** (-jnp.arange(0, half, 2, dtype=F32) / half)
    ang = jnp.concatenate([r[:, None] * inv, col[:, None] * inv], axis=-1)
    cos, sin = jnp.cos(ang), jnp.sin(ang)
    lat = jnp.concatenate([cos, cos, -sin, sin], axis=-1)
    ident = jnp.concatenate([jnp.ones((ctx_len, MLA_ROPE), F32), jnp.zeros((ctx_len, MLA_ROPE), F32)], axis=-1)
    return jnp.concatenate([jnp.tile(lat, (batch, 1)), jnp.tile(ident, (batch, 1))], axis=0)


_ROPE_SWAP = np.concatenate([np.arange(MLA_ROPE // 2, MLA_ROPE), np.arange(MLA_ROPE // 2)])


def _mla_weights(w_in, w_uq, w_ukv):
    lo = Q_LORA + KV_LORA
    k_rope = w_in[:, lo:lo + MLA_ROPE]
    w_in_p = jnp.concatenate([w_in[:, :lo], w_in[:, lo + MLA_ROPE:], k_rope, k_rope[:, _ROPE_SWAP]], axis=1)
    head = np.concatenate([np.arange(MLA_QK), MLA_NOPE + _ROPE_SWAP])
    cols = (np.arange(MLA_HEADS)[:, None] * MLA_QK + head[None, :]).reshape(-1)
    wqt = w_uq[:, cols].T.astype(MXU_DTYPE)
    w_kv = w_ukv.reshape(KV_LORA, MLA_HEADS, MLA_NOPE + MLA_V)
    wk = w_kv[:, :, :MLA_NOPE].reshape(KV_LORA, MLA_HEADS * MLA_NOPE).astype(MXU_DTYPE)
    wvt = w_kv[:, :, MLA_NOPE:].reshape(KV_LORA, MLA_WIDTH).T.astype(MXU_DTYPE)
    return w_in_p, wqt, wk, wvt


def _mla_layer(h, a, mod, tab, tabt, w_in, q_norm, w_uq, kv_norm, w_ukv, w_o_all, layer, geom):
    w_in_p, wqt, wk, wvt = _mla_weights(w_in, w_uq, w_ukv)
    proj = _matmul(a, w_in_p[None], 0, tn=640, name="mla_in")
    qt, k, vt = _mla_up(proj, q_norm, kv_norm, wqt, wk, wvt, tab, tabt)
    gated = _attention(qt, k, vt, proj, geom)
    return _matmul_residual(gated, w_o_all, layer, h, mod, geom, tn=1024, name="mla_out")


def _gdn_layer(h, a, mod, w_in_all, conv_w, a_log, dt_bias, o_norm, w_o_all, layer, geom):
    proj = _matmul(a, w_in_all, layer, tn=1024, n_blocks=GDN_QKVZ // 1024, name="gdn_in")
    ba = _matmul(a, w_in_all, layer, tn=LANES, first_block=GDN_QKVZ // LANES, n_blocks=1, name="gdn_in_gates")
    qkv = _gdn_conv(proj, conv_w, geom)
    gb, gbt = _gdn_gates(ba, a_log, dt_bias)
    gated = _gdn_scan(qkv, gb, gbt, proj, o_norm, geom)
    return _matmul_residual(gated, w_o_all, layer, h, mod, geom, name="gdn_out")


def kernel(x, c, ctx, c_ctx, ada_w, ada_b, norm_w, mla_w_in, mla_q_norm, mla_w_uq, mla_kv_norm, mla_w_ukv, mla_w_o, gdn_w_in, gdn_conv_w, gdn_a_log, gdn_dt_bias, gdn_o_norm, gdn_w_o, final_norm):
    batch, seq, d = x.shape
    ctx_len = ctx.shape[1]
    n_rows = batch * (seq + ctx_len)
    assert d == D_MODEL and seq % ATTN_TQ == 0 and ctx_len % ROW_TILE == 0 and seq % GRID_W == 0
    assert batch < SUBLANES and n_rows % MM_TM == 0 and (batch * seq) % ctx_len == 0
    geom = Geom(batch=batch, n_lat=seq // ROW_TILE, n_ctx=ctx_len // ROW_TILE)

    h = jnp.concatenate([x.reshape(batch * seq, d), ctx.reshape(batch * ctx_len, d)], axis=0)
    conds = jnp.zeros((SUBLANES, d), F32).at[:batch].set(c).at[batch].set(c_ctx)
    mods = _adaln(conds, ada_w, ada_b)
    tab = _rope_table(batch, ctx_len, seq)
    tabt = tab.T

    for i in range(DEPTH):
        a = _prenorm(h, norm_w[i], mods[i], geom)
        j = i // 2
        if i % 2 == 0:
            h = _mla_layer(h, a, mods[i], tab, tabt, mla_w_in[j], mla_q_norm[j], mla_w_uq[j], mla_kv_norm[j],
                           mla_w_ukv[j], mla_w_o, j, geom)
        else:
            h = _gdn_layer(h, a, mods[i], gdn_w_in, gdn_conv_w[j], gdn_a_log[j], gdn_dt_bias[j],
                           gdn_o_norm[j], gdn_w_o, j, geom)
    return _final_norm(h, final_norm, geom).reshape(batch, seq, d)
```
